```python
import functools
import jax, jax.numpy as jnp
from jax import lax
import numpy as np

D_MODEL = 1024
BATCH = 1
SEQ = 16384
DEPTH = 1
DEC_BATCH = 32
DEC_SEQ = 4
PAST_LEN = 16384
PAGE_SIZE = 128

D_CONV = D_MODEL // 2
CONV_WIDTH = 31
N_HEADS = 8
HEAD_DIM = 64
D_ATTN = N_HEADS * HEAD_DIM
IDX_HEADS = 8
IDX_DIM = 64
TOPK_MAX = 256
D_FF = 2816
FFN_CONV_WIDTH = 3
Q_BLOCK = 128
EPS = 1e-6
NEG = -1e30
INDEX_WEIGHT_SCALE = (IDX_HEADS * IDX_DIM) ** -0.5
IN_SIZES = (D_CONV, D_CONV, D_ATTN, D_ATTN, D_ATTN, IDX_HEADS * IDX_DIM, IDX_DIM, IDX_HEADS)
IN_SPLITS = tuple(int(s) for s in np.cumsum(IN_SIZES)[:-1])
N_IN = sum(IN_SIZES)

kernel_name = 'hymba_conformer_dsa_convffn_step'


def rms_norm(x, g):
    xf = x.astype(jnp.float32)
    y = xf * lax.rsqrt(jnp.mean(xf * xf, axis=-1, keepdims=True) + EPS)
    return (y * g.astype(jnp.float32)).astype(x.dtype)


def layer_norm(x, g, b):
    xf = x.astype(jnp.float32)
    mu = jnp.mean(xf, axis=-1, keepdims=True)
    var = jnp.mean(jnp.square(xf - mu), axis=-1, keepdims=True)
    y = (xf - mu) * lax.rsqrt(var + EPS)
    return (y * g.astype(jnp.float32) + b.astype(jnp.float32)).astype(x.dtype)


def causal_dwconv(xp, w, b):
    c = xp.shape[-1]
    y = lax.conv_general_dilated(xp, w[:, None, :].astype(xp.dtype), window_strides=(1,), padding='VALID',
                                 dimension_numbers=('NWC', 'WIO', 'NWC'), feature_group_count=c)
    return y + b


def gather_rows(arr, idx):
    return jax.vmap(lambda a, i: a[i])(arr, idx)


def index_scores(qi, wi, ki, mask):
    s = jax.nn.relu(jnp.einsum('bthd,bld->bthl', qi, ki).astype(jnp.float32))
    score = jnp.einsum('bthl,bth->btl', s, wi.astype(jnp.float32))
    return jnp.where(mask, score, NEG)


def attend(q, ks, vs, valid):
    s = jnp.einsum('bthd,btkhd->bthk', q, ks).astype(jnp.float32) * HEAD_DIM ** -0.5
    p = jax.nn.softmax(jnp.where(valid[:, :, None, :], s, NEG), axis=-1)
    o = jnp.einsum('bthk,btkhd->bthd', p.astype(vs.dtype), vs)
    return o.reshape(q.shape[0], q.shape[1], D_ATTN)


def prompt_sparse_attention(q, k, v, qi, wi, ki):
    b, s = q.shape[0], q.shape[1]
    topk = min(TOPK_MAX, s // 4)
    key_pos = jnp.arange(s)

    def block(start):
        qb = lax.dynamic_slice_in_dim(q, start, Q_BLOCK, axis=1)
        qib = lax.dynamic_slice_in_dim(qi, start, Q_BLOCK, axis=1)
        wib = lax.dynamic_slice_in_dim(wi, start, Q_BLOCK, axis=1)
        qpos = start + jnp.arange(Q_BLOCK)
        mask = key_pos[None, :] <= qpos[:, None]
        score = index_scores(qib, wib, ki, mask[None])
        _, sel = lax.top_k(score, topk)
        valid = sel <= qpos[None, :, None]
        return attend(qb, gather_rows(k, sel), gather_rows(v, sel), valid)

    out = lax.map(block, jnp.arange(s // Q_BLOCK) * Q_BLOCK)
    return out.transpose(1, 0, 2, 3).reshape(b, s, D_ATTN)


def sample_sparse_attention(q, k, v, qi, wi, ki, *, cache_k, cache_v, cache_kidx, page_table, layer):
    db, t = q.shape[0], q.shape[1]
    past = page_table.shape[1] * PAGE_SIZE
    total = past + t
    topk = min(TOPK_MAX, total // 4)
    ki_past = cache_kidx[layer, page_table].reshape(db, past, IDX_DIM)
    ki_all = jnp.concatenate([ki_past, ki], axis=1)
    qpos = past + jnp.arange(t)
    mask = jnp.arange(total)[None, :] <= qpos[:, None]
    score = index_scores(qi, wi, ki_all, mask[None])
    _, sel = lax.top_k(score, topk)
    is_past = (sel < past)[..., None, None]
    pidx = jnp.minimum(sel, past - 1)
    phys = gather_rows(page_table, pidx // PAGE_SIZE)
    off = pidx % PAGE_SIZE
    nidx = jnp.clip(sel - past, 0, t - 1)
    ks = jnp.where(is_past, cache_k[layer, phys, off], gather_rows(k, nidx))
    vs = jnp.where(is_past, cache_v[layer, phys, off], gather_rows(v, nidx))
    valid = sel <= qpos[None, :, None]
    return attend(q, ks, vs, valid)


def hybrid_layer(x, c, conv_prev, ffn_prev, attention, w_ada, b_ada, g_pre_mix, w_in, w_dw, b_dw, ln_g, ln_b,
                 w_out, g_post_mix, g_pre_ffn, w_up, w_ffn_dw, b_ffn_dw, w_down, g_post_ffn):
    b, t, _ = x.shape
    mod = jax.nn.silu(c) @ w_ada + b_ada
    shift1, scale1, gate1, shift2, scale2, gate2 = [m[:, None, :] for m in jnp.split(mod, 6, axis=-1)]
    h = rms_norm(x, g_pre_mix) * (1 + scale1) + shift1
    ca, cg, q, k, v, qi, ki, wi = jnp.split(h @ w_in, IN_SPLITS, axis=-1)
    glu = ca * jax.nn.sigmoid(cg)
    conv_in = jnp.concatenate([conv_prev, glu], axis=1)
    conv_y = jax.nn.silu(layer_norm(causal_dwconv(conv_in, w_dw, b_dw), ln_g, ln_b))
    q = q.reshape(b, t, N_HEADS, HEAD_DIM)
    k = k.reshape(b, t, N_HEADS, HEAD_DIM)
    v = v.reshape(b, t, N_HEADS, HEAD_DIM)
    qi = qi.reshape(b, t, IDX_HEADS, IDX_DIM)
    attn_y = attention(q, k, v, qi, wi * INDEX_WEIGHT_SCALE, ki)
    mix = jnp.concatenate([conv_y, attn_y], axis=-1) @ w_out
    x = x + gate1 * rms_norm(mix, g_post_mix)
    h2 = rms_norm(x, g_pre_ffn) * (1 + scale2) + shift2
    ffn_in = jnp.concatenate([ffn_prev, h2 @ w_up], axis=1)
    a, g = jnp.split(causal_dwconv(ffn_in, w_ffn_dw, b_ffn_dw), 2, axis=-1)
    f = (jax.nn.silu(g) * a) @ w_down
    x = x + gate2 * rms_norm(f, g_post_ffn)
    return x, k, v, ki, conv_in[:, -(CONV_WIDTH - 1):], ffn_in[:, -(FFN_CONV_WIDTH - 1):]


def setup_inputs(seed: int = 0) -> dict:
    key = jax.random.key(seed)
    ks = jax.random.split(key, 32)
    n_pages = PAST_LEN // PAGE_SIZE
    n_phys = (5 * DEC_BATCH * n_pages + 3) // 4
    f32 = jnp.float32
    nrm = lambda k, shape, s=1.0: (jax.random.normal(k, shape, f32) * s).astype(f32)
    perm = jax.random.permutation(ks[7], n_phys).astype(jnp.int32)
    page_table = perm[:DEC_BATCH * n_pages].reshape(DEC_BATCH, n_pages)
    return {
        'x_prompt': nrm(ks[0], (BATCH, SEQ, D_MODEL)),
        'x_sample': nrm(ks[1], (DEC_BATCH, DEC_SEQ, D_MODEL)),
        'cache_k': nrm(ks[2], (DEPTH, n_phys, PAGE_SIZE, N_HEADS, HEAD_DIM)),
        'cache_v': nrm(ks[3], (DEPTH, n_phys, PAGE_SIZE, N_HEADS, HEAD_DIM)),
        'cache_kidx': nrm(ks[4], (DEPTH, n_phys, PAGE_SIZE, IDX_DIM)),
        'state_conv': nrm(ks[5], (DEPTH, DEC_BATCH, CONV_WIDTH - 1, D_CONV), 0.5),
        'state_ffn': nrm(ks[6], (DEPTH, DEC_BATCH, FFN_CONV_WIDTH - 1, 2 * D_FF)),
        'page_table': page_table,
        'c_prompt': nrm(ks[8], (BATCH, D_MODEL)),
        'c_sample': nrm(ks[9], (DEC_BATCH, D_MODEL)),
        'w_ada': nrm(ks[10], (DEPTH, D_MODEL, 6 * D_MODEL), 0.5 * D_MODEL ** -0.5),
        'b_ada': nrm(ks[11], (DEPTH, 6 * D_MODEL), 0.01),
        'g_pre_mix': 1.0 + nrm(ks[12], (DEPTH, D_MODEL), 0.01),
        'w_in': nrm(ks[13], (DEPTH, D_MODEL, N_IN), D_MODEL ** -0.5),
        'w_dw': nrm(ks[14], (DEPTH, CONV_WIDTH, D_CONV), CONV_WIDTH ** -0.5),
        'b_dw': nrm(ks[15], (DEPTH, D_CONV), 0.01),
        'ln_g': 1.0 + nrm(ks[16], (DEPTH, D_CONV), 0.01),
        'ln_b': nrm(ks[17], (DEPTH, D_CONV), 0.01),
        'w_out': nrm(ks[18], (DEPTH, D_CONV + D_ATTN, D_MODEL), (D_CONV + D_ATTN) ** -0.5),
        'g_post_mix': 1.0 + nrm(ks[19], (DEPTH, D_MODEL), 0.01),
        'g_pre_ffn': 1.0 + nrm(ks[20], (DEPTH, D_MODEL), 0.01),
        'w_up': nrm(ks[21], (DEPTH, D_MODEL, 2 * D_FF), D_MODEL ** -0.5),
        'w_ffn_dw': nrm(ks[22], (DEPTH, FFN_CONV_WIDTH, 2 * D_FF), FFN_CONV_WIDTH ** -0.5),
        'b_ffn_dw': nrm(ks[23], (DEPTH, 2 * D_FF), 0.01),
        'w_down': nrm(ks[24], (DEPTH, D_FF, D_MODEL), D_FF ** -0.5),
        'g_post_ffn': 1.0 + nrm(ks[25], (DEPTH, D_MODEL), 0.01),
    }


def reference(x_prompt, x_sample, cache_k, cache_v, cache_kidx, state_conv, state_ffn, page_table, c_prompt, c_sample,
              w_ada, b_ada, g_pre_mix, w_in, w_dw, b_dw, ln_g, ln_b, w_out, g_post_mix, g_pre_ffn, w_up, w_ffn_dw,
              b_ffn_dw, w_down, g_post_ffn):
    xp, xs = x_prompt, x_sample
    kp_l, vp_l, kip_l, cp_l, fp_l = [], [], [], [], []
    ks_l, vs_l, kis_l, cs_l, fs_l = [], [], [], [], []
    for l in range(DEPTH):
        lw = (w_ada[l], b_ada[l], g_pre_mix[l], w_in[l], w_dw[l], b_dw[l], ln_g[l], ln_b[l], w_out[l],
              g_post_mix[l], g_pre_ffn[l], w_up[l], w_ffn_dw[l], b_ffn_dw[l], w_down[l], g_post_ffn[l])
        conv0 = jnp.zeros((xp.shape[0], CONV_WIDTH - 1, D_CONV), xp.dtype)
        ffn0 = jnp.zeros((xp.shape[0], FFN_CONV_WIDTH - 1, 2 * D_FF), xp.dtype)
        xp, kp, vp, kip, cp, fp = hybrid_layer(xp, c_prompt, conv0, ffn0, prompt_sparse_attention, *lw)
        attn_s = functools.partial(sample_sparse_attention, cache_k=cache_k, cache_v=cache_v,
                                   cache_kidx=cache_kidx, page_table=page_table, layer=l)
        xs, k_s, v_s, ki_s, c_s, f_s = hybrid_layer(xs, c_sample, state_conv[l], state_ffn[l], attn_s, *lw)
        kp_l.append(kp); vp_l.append(vp); kip_l.append(kip); cp_l.append(cp); fp_l.append(fp)
        ks_l.append(k_s); vs_l.append(v_s); kis_l.append(ki_s); cs_l.append(c_s); fs_l.append(f_s)
    return (xp, xs, jnp.stack(kp_l), jnp.stack(vp_l), jnp.stack(kip_l), jnp.stack(cp_l), jnp.stack(fp_l),
            jnp.stack(ks_l), jnp.stack(vs_l), jnp.stack(kis_l), jnp.stack(cs_l), jnp.stack(fs_l))
```

```python
import functools

import jax
import jax.numpy as jnp
import numpy as np
from jax import lax
from jax.experimental import pallas as pl
from jax.experimental.pallas import tpu as pltpu

F32 = jnp.float32
BF16 = jnp.bfloat16
I32 = jnp.int32

D_MODEL = 1024
D_CONV = 512
CONV_WIDTH = 31
N_HEADS = 8
HEAD_DIM = 64
D_ATTN = N_HEADS * HEAD_DIM
IDX_HEADS = 8
IDX_DIM = 64
TOPK_MAX = 256
D_FF = 2816
FFN_CONV_WIDTH = 3
PAGE_SIZE = 128
EPS = 1e-6
NEG = -1e30
INDEX_WEIGHT_SCALE = (IDX_HEADS * IDX_DIM) ** -0.5
N_MAIN = 6 * 512
INT_MIN = -(2 ** 31)

LANES = 128
SUBLANES = 8
VMEM_LIMIT = 56 * 1024 * 1024

QB = 128
SUB = 256
KT = 1024
PPC = 16


def _cparams(sem):
    return pltpu.CompilerParams(dimension_semantics=sem, vmem_limit_bytes=VMEM_LIMIT)


def _rms(x, g):
    return x * lax.rsqrt(jnp.mean(x * x, axis=-1, keepdims=True) + EPS) * g


def _dot(a, b):
    return jnp.dot(a, b, preferred_element_type=F32)


def _dot_nt(a, b):
    return lax.dot_general(a, b, (((1,), (1,)), ((), ())), preferred_element_type=F32)


def _ada_kernel(c_ref, w_ref, b_ref, o_ref):
    c = c_ref[...]
    s = (c * jax.nn.sigmoid(c)).astype(BF16)
    o_ref[...] = _dot(s, w_ref[...].astype(BF16)) + b_ref[...]


def _ada(c_all, w_ada, b_ada):
    r, d = c_all.shape
    n = w_ada.shape[1]
    tn = 1024
    return pl.pallas_call(
        _ada_kernel,
        grid=(n // tn,),
        in_specs=[pl.BlockSpec((r, d), lambda j: (0, 0)),
                  pl.BlockSpec((d, tn), lambda j: (0, j)),
                  pl.BlockSpec((1, tn), lambda j: (0, j))],
        out_specs=pl.BlockSpec((r, tn), lambda j: (0, j)),
        out_shape=jax.ShapeDtypeStruct((r, n), F32),
        compiler_params=_cparams(("arbitrary",)),
        name="ada",
    )(c_all, w_ada, b_ada.reshape(1, n))


def _inproj_kernel(x_ref, shift_ref, scale_ref, g_ref, wm_ref, wt_ref,
                   glu_ref, q_ref, k_ref, kb_ref, v_ref, vb_ref, qi_ref, tail_ref):
    h = _rms(x_ref[...], g_ref[...]) * (1.0 + scale_ref[...]) + shift_ref[...]
    hb = h.astype(BF16)

    def proj(i):
        return _dot(hb, wm_ref[:, i * 512:(i + 1) * 512])

    ca = proj(0)
    cg = proj(1)
    glu_ref[...] = ca * jax.nn.sigmoid(cg)
    q_ref[...] = (proj(2) * HEAD_DIM ** -0.5).astype(BF16)
    k = proj(3)
    k_ref[...] = k
    kb_ref[...] = k.astype(BF16)
    v = proj(4)
    v_ref[...] = v
    vb_ref[...] = v.astype(BF16)
    qi_ref[...] = proj(5).astype(BF16)
    lane = lax.broadcasted_iota(I32, (1, LANES), 1)
    tail_ref[...] = _dot(hb, wt_ref[...]) * jnp.where(lane >= IDX_DIM, INDEX_WEIGHT_SCALE, 1.0)


def _inproj(x, shift, scale, g, w_main, w_tail, tm):
    r, d = x.shape
    per_row = shift.shape[0] != 1
    mod_spec = pl.BlockSpec((tm, d), lambda i: (i, 0)) if per_row else pl.BlockSpec((1, d), lambda i: (0, 0))
    row = lambda n: pl.BlockSpec((tm, n), lambda i: (i, 0))
    full = lambda a: pl.BlockSpec(a.shape, lambda i: (0, 0))
    sds = lambda n, dt: jax.ShapeDtypeStruct((r, n), dt)
    return pl.pallas_call(
        _inproj_kernel,
        grid=(r // tm,),
        in_specs=[row(d), mod_spec, mod_spec, full(g), full(w_main), full(w_tail)],
        out_specs=[row(512)] * 7 + [row(LANES)],
        out_shape=[sds(512, F32), sds(512, BF16), sds(512, F32), sds(512, BF16), sds(512, F32), sds(512, BF16),
                   sds(512, BF16), sds(LANES, F32)],
        compiler_params=_cparams(("arbitrary",)),
        name="inproj",
    )(x, shift, scale, g, w_main, w_tail)


def _fill_window(win_ref, hist_ref, halo_ref, main_ref, hb, tm):
    win_ref[hb:hb + tm, :] = main_ref[...]
    if halo_ref is None:
        win_ref[0:hb, :] = hist_ref[...]
    else:
        first = pl.program_id(0) == 0

        @pl.when(first)
        def _():
            win_ref[0:hb, :] = hist_ref[...]

        @pl.when(jnp.logical_not(first))
        def _():
            win_ref[0:hb, :] = halo_ref[...]


def _window_specs(hist, main_cols, tm, hb, n_tiles):
    specs = [pl.BlockSpec(hist.shape, lambda i: (0, 0))]
    if n_tiles > 1:
        per = tm // hb
        specs.append(pl.BlockSpec((hb, main_cols), lambda i: (jnp.maximum(i * per - 1, 0), 0)))
    specs.append(pl.BlockSpec((tm, main_cols), lambda i: (i, 0)))
    return specs


def _conv_ln_kernel(*refs, taps, bs, tm, hb, rc, multi):
    if multi:
        hist_ref, halo_ref, main_ref, w_ref, b_ref, g_ref, beta_ref, o_ref, win_ref = refs
    else:
        hist_ref, main_ref, w_ref, b_ref, g_ref, beta_ref, o_ref, win_ref = refs
        halo_ref = None
    _fill_window(win_ref, hist_ref, halo_ref, main_ref, hb, tm)
    base = hb - (taps - 1) * bs
    c = o_ref.shape[1]
    for ci in range(tm // rc):
        acc = jnp.broadcast_to(b_ref[...], (rc, c))
        for j in range(taps):
            r0 = base + ci * rc + j * bs
            acc = acc + w_ref[j:j + 1, :] * win_ref[r0:r0 + rc, :]
        mu = jnp.mean(acc, axis=-1, keepdims=True)
        xc = acc - mu
        var = jnp.mean(xc * xc, axis=-1, keepdims=True)
        y = xc * lax.rsqrt(var + EPS) * g_ref[...] + beta_ref[...]
        o_ref[ci * rc:(ci + 1) * rc, :] = y * jax.nn.sigmoid(y)


def _conv_ln(hist, glu, w, b, g, beta, *, bs, tm, rc):
    r, c = glu.shape
    taps = w.shape[0]
    hb = hist.shape[0]
    n_tiles = r // tm
    vec = lambda a: pl.BlockSpec(a.shape, lambda i: (0, 0))
    kern = functools.partial(_conv_ln_kernel, taps=taps, bs=bs, tm=tm, hb=hb, rc=rc, multi=n_tiles > 1)
    args = [hist] + ([glu] if n_tiles > 1 else []) + [glu, w, b, g, beta]
    return pl.pallas_call(
        kern,
        grid=(n_tiles,),
        in_specs=_window_specs(hist, c, tm, hb, n_tiles) + [vec(w), vec(b), vec(g), vec(beta)],
        out_specs=pl.BlockSpec((tm, c), lambda i: (i, 0)),
        out_shape=jax.ShapeDtypeStruct((r, c), F32),
        scratch_shapes=[pltpu.VMEM((hb + tm, c), F32)],
        compiler_params=_cparams(("arbitrary",)),
        name="conv_ln",
    )(*args)


def _sortable(x):
    bits = pltpu.bitcast(x, I32)
    return bits ^ ((bits >> 31) & 0x7FFFFFFF)


def _select_topk(keys_ref, nsub, limit, topk, demote_from):
    zero8 = jnp.zeros((SUBLANES, LANES), I32)

    def count(pred):
        def body(s, acc):
            r0 = pl.multiple_of(s * SUB, SUB)
            kk = keys_ref[pl.ds(r0, SUB), :]
            hit = jnp.where(pred(kk, r0), 1, 0)
            return acc + hit.reshape(SUB // SUBLANES, SUBLANES, LANES).sum(axis=0)
        acc = lax.fori_loop(0, nsub, body, zero8)
        return jnp.sum(acc, axis=0, keepdims=True)

    def count_ge(cand):
        return count(lambda kk, r0: kk >= cand)

    tau = jnp.where(count_ge(jnp.zeros((1, LANES), I32)) >= topk, 0, INT_MIN).astype(I32)

    def bit_body(b, tau):
        cand = tau | jnp.left_shift(jnp.int32(1), 30 - b)
        return jnp.where(count_ge(cand) >= topk, cand, tau)

    tau = lax.fori_loop(0, 31, bit_body, tau)
    n_ge = count_ge(tau)

    row_iota = lax.broadcasted_iota(I32, (SUB, LANES), 0)

    @pl.when(jnp.max(n_ge) > topk)
    def _ties():
        n_gt = count(lambda kk, r0: kk > tau)
        keep = topk - n_gt

        def pos_body(b, q):
            cand = q | jnp.left_shift(jnp.int32(1), 30 - b)
            n = count(lambda kk, r0: jnp.logical_and(kk == tau, (r0 + row_iota) < cand))
            return jnp.where(n < keep, cand, q)

        last = lax.fori_loop(0, 31, pos_body, jnp.zeros((1, LANES), I32))
        lower = jnp.where(tau == INT_MIN, INT_MIN, tau - 1)

        def demote(s, c):
            r0 = pl.multiple_of(s * SUB, SUB)
            kk = keys_ref[pl.ds(r0, SUB), :]
            drop = jnp.logical_and(kk == tau, (r0 + row_iota) > last)
            keys_ref[pl.ds(r0, SUB), :] = jnp.where(drop, lower, kk)
            return c

        lax.fori_loop(0, nsub, demote, 0)

    def causal(s, c):
        r0 = pl.multiple_of(s * SUB, SUB)
        kk = keys_ref[pl.ds(r0, SUB), :]
        keys_ref[pl.ds(r0, SUB), :] = jnp.where((r0 + row_iota) <= limit, kk, INT_MIN)
        return c

    lax.fori_loop(demote_from, nsub, causal, 0)
    return jnp.maximum(tau, INT_MIN + 1)


def _attn_prompt_kernel(qb_tab, kt_tab, qall_ref, wt_ref, ki_ref, qbd_ref, k_ref, vt_ref, o_ref,
                        keys_ref, thr_ref, m_ref, l_ref, acc_ref, *, topk):
    i = pl.program_id(0)
    qb = qb_tab[i]
    kt = kt_tab[i]
    q_end = (qb + 1) * QB

    @pl.when(kt == 0)
    def _select():
        nsub = (q_end + SUB - 1) // SUB
        qpos = qb * QB + lax.broadcasted_iota(I32, (1, LANES), 1)
        row_iota = lax.broadcasted_iota(I32, (SUB, LANES), 0)

        def scores(s, c):
            r0 = pl.multiple_of(s * SUB, SUB)
            kis = ki_ref[pl.ds(r0, SUB), :]
            acc = jnp.zeros((SUB, LANES), F32)
            for p in range(IDX_HEADS // 2):
                r = _dot_nt(kis, qall_ref[p * 2 * QB:(p + 1) * 2 * QB, :])
                acc = acc + wt_ref[2 * p:2 * p + 1, :] * jnp.maximum(r[:, :QB], 0.0)
                acc = acc + wt_ref[2 * p + 1:2 * p + 2, :] * jnp.maximum(r[:, QB:], 0.0)
            sc = jnp.where((r0 + row_iota) <= qpos, acc, NEG)
            keys_ref[pl.ds(r0, SUB), :] = _sortable(sc)
            return c

        lax.fori_loop(0, nsub, scores, 0)
        thr = _select_topk(keys_ref, nsub, qpos, topk, (qb * QB) // SUB)
        thr_ref[...] = jnp.broadcast_to(thr, thr_ref.shape)

        def clear(s, c):
            r0 = pl.multiple_of(s * SUB, SUB)
            keys_ref[pl.ds(r0, SUB), :] = jnp.full((SUB, LANES), INT_MIN, I32)
            return c

        lax.fori_loop(nsub, ((q_end + KT - 1) // KT) * (KT // SUB), clear, 0)
        m_ref[...] = jnp.full(m_ref.shape, NEG, F32)
        l_ref[...] = jnp.zeros(l_ref.shape, F32)
        acc_ref[...] = jnp.zeros(acc_ref.shape, F32)

    thr = thr_ref[0:1, :]
    for s in range(KT // SUB):
        r0 = pl.multiple_of(kt * KT + s * SUB, SUB)
        sel = keys_ref[pl.ds(r0, SUB), :] >= thr
        for p in range(N_HEADS // 2):
            st = _dot_nt(k_ref[s * SUB:(s + 1) * SUB, p * LANES:(p + 1) * LANES], qbd_ref[p])
            for hh in range(2):
                h = 2 * p + hh
                sc = jnp.where(sel, st[:, hh * QB:(hh + 1) * QB], 2 * NEG)
                m_old = m_ref[h:h + 1, :]
                m_new = jnp.maximum(m_old, jnp.max(sc, axis=0, keepdims=True))
                alpha = jnp.exp(m_old - m_new)
                pm = jnp.exp(sc - m_new)
                l_ref[h:h + 1, :] = alpha * l_ref[h:h + 1, :] + jnp.sum(pm, axis=0, keepdims=True)
                pv = _dot(vt_ref[h * HEAD_DIM:(h + 1) * HEAD_DIM, s * SUB:(s + 1) * SUB], pm.astype(BF16))
                acc_ref[h * HEAD_DIM:(h + 1) * HEAD_DIM, :] = alpha * acc_ref[h * HEAD_DIM:(h + 1) * HEAD_DIM, :] + pv
                m_ref[h:h + 1, :] = m_new

    @pl.when((kt + 1) * KT >= q_end)
    def _finish():
        for h in range(N_HEADS):
            rows = slice(h * HEAD_DIM, (h + 1) * HEAD_DIM)
            acc_ref[rows, :] = acc_ref[rows, :] / l_ref[h:h + 1, :]
        o_ref[...] = acc_ref[...].T


def _attn_prompt(qi_b, wi, ki_b, q_b, k_b, v_b, topk):
    t = qi_b.shape[0]
    nb = t // QB
    qall = qi_b.reshape(nb, QB, IDX_HEADS, IDX_DIM).transpose(0, 2, 1, 3).reshape(nb, IDX_HEADS * QB, IDX_DIM)
    qh = q_b.reshape(nb, QB, N_HEADS // 2, 2, HEAD_DIM).transpose(0, 2, 3, 1, 4)
    zero = jnp.zeros_like(qh[:, :, 0])
    qbd = jnp.concatenate([jnp.concatenate([qh[:, :, 0], zero], axis=-1),
                           jnp.concatenate([zero, qh[:, :, 1]], axis=-1)], axis=2)
    wt = wi.T
    vt = v_b.T

    steps = [(b, j) for b in range(nb) for j in range(((b + 1) * QB + KT - 1) // KT)]
    qb_tab = jnp.asarray(np.array([s[0] for s in steps], np.int32))
    kt_tab = jnp.asarray(np.array([s[1] for s in steps], np.int32))
    t_pad = ((t + KT - 1) // KT) * KT

    grid_spec = pltpu.PrefetchScalarGridSpec(
        num_scalar_prefetch=2,
        grid=(len(steps),),
        in_specs=[
            pl.BlockSpec((None, IDX_HEADS * QB, IDX_DIM), lambda i, qb, kt: (qb[i], 0, 0)),
            pl.BlockSpec((IDX_HEADS, QB), lambda i, qb, kt: (0, qb[i])),
            pl.BlockSpec((t, IDX_DIM), lambda i, qb, kt: (0, 0)),
            pl.BlockSpec((None, N_HEADS // 2, 2 * QB, LANES), lambda i, qb, kt: (qb[i], 0, 0, 0)),
            pl.BlockSpec((KT, D_ATTN), lambda i, qb, kt: (kt[i], 0)),
            pl.BlockSpec((D_ATTN, KT), lambda i, qb, kt: (0, kt[i])),
        ],
        out_specs=pl.BlockSpec((QB, D_ATTN), lambda i, qb, kt: (qb[i], 0)),
        scratch_shapes=[
            pltpu.VMEM((t_pad, LANES), I32),
            pltpu.VMEM((SUBLANES, LANES), I32),
            pltpu.VMEM((N_HEADS, LANES), F32),
            pltpu.VMEM((N_HEADS, LANES), F32),
            pltpu.VMEM((D_ATTN, LANES), F32),
        ],
    )
    return pl.pallas_call(
        functools.partial(_attn_prompt_kernel, topk=topk),
        grid_spec=grid_spec,
        out_shape=jax.ShapeDtypeStruct((t, D_ATTN), F32),
        compiler_params=_cparams(("arbitrary",)),
        name="attn_prompt",
    )(qb_tab, kt_tab, qall, wt, ki_b, qbd, k_b, vt)


def _mix_up_kernel(x_ref, cy_ref, ay_ref, gate_ref, shift_ref, scale_ref, gpm_ref, gpf_ref, wo_ref, wu_ref,
                   x1_ref, up_ref):
    mix = _dot(cy_ref[...].astype(BF16), wo_ref[0:D_CONV, :]) + _dot(ay_ref[...].astype(BF16), wo_ref[D_CONV:, :])
    x1 = x_ref[...] + gate_ref[...] * _rms(mix, gpm_ref[...])
    x1_ref[...] = x1
    h2 = (_rms(x1, gpf_ref[...]) * (1.0 + scale_ref[...]) + shift_ref[...]).astype(BF16)
    for c in range(up_ref.shape[1] // 512):
        up_ref[:, c * 512:(c + 1) * 512] = _dot(h2, wu_ref[:, c * 512:(c + 1) * 512])


def _mix_up(x, cy, ay, gate1, shift2, scale2, g_post_mix, g_pre_ffn, w_out, w_up, tm):
    r, d = x.shape
    nf = w_up.shape[1]
    per_row = gate1.shape[0] != 1
    mod_spec = pl.BlockSpec((tm, d), lambda i: (i, 0)) if per_row else pl.BlockSpec((1, d), lambda i: (0, 0))
    row = lambda n: pl.BlockSpec((tm, n), lambda i: (i, 0))
    full = lambda a: pl.BlockSpec(a.shape, lambda i: (0, 0))
    return pl.pallas_call(
        _mix_up_kernel,
        grid=(r // tm,),
        in_specs=[row(d), row(D_CONV), row(D_ATTN), mod_spec, mod_spec, mod_spec, full(g_post_mix), full(g_pre_ffn),
                  full(w_out), full(w_up)],
        out_specs=[row(d), row(nf)],
        out_shape=[jax.ShapeDtypeStruct((r, d), F32), jax.ShapeDtypeStruct((r, nf), F32)],
        compiler_params=_cparams(("arbitrary",)),
        name="mix_up",
    )(x, cy, ay, gate1, shift2, scale2, g_post_mix, g_pre_ffn, w_out, w_up)


def _ffn_down_kernel(*refs, bs, tm, hb, multi):
    if multi:
        hist_ref, halo_ref, main_ref, x1_ref, gate_ref, w_ref, b_ref, wd_ref, g_ref, o_ref, win_ref = refs
    else:
        hist_ref, main_ref, x1_ref, gate_ref, w_ref, b_ref, wd_ref, g_ref, o_ref, win_ref = refs
        halo_ref = None
    _fill_window(win_ref, hist_ref, halo_ref, main_ref, hb, tm)
    taps = w_ref.shape[0]
    base = hb - (taps - 1) * bs
    cw = 256

    def conv(c0):
        acc = jnp.broadcast_to(b_ref[:, c0:c0 + cw], (tm, cw))
        for j in range(taps):
            acc = acc + w_ref[j:j + 1, c0:c0 + cw] * win_ref[base + j * bs:base + j * bs + tm, c0:c0 + cw]
        return acc

    f = jnp.zeros((tm, o_ref.shape[1]), F32)
    for c in range(D_FF // cw):
        a = conv(c * cw)
        g = conv(D_FF + c * cw)
        gated = (g * jax.nn.sigmoid(g) * a).astype(BF16)
        f = f + _dot(gated, wd_ref[c * cw:(c + 1) * cw, :])
    o_ref[...] = x1_ref[...] + gate_ref[...] * _rms(f, g_ref[...])


def _ffn_down(hist, up, x1, gate2, w, b, w_down, g, *, bs, tm):
    r, nf = up.shape
    d = x1.shape[1]
    hb = hist.shape[0]
    n_tiles = r // tm
    per_row = gate2.shape[0] != 1
    mod_spec = pl.BlockSpec((tm, d), lambda i: (i, 0)) if per_row else pl.BlockSpec((1, d), lambda i: (0, 0))
    full = lambda a: pl.BlockSpec(a.shape, lambda i: (0, 0))
    kern = functools.partial(_ffn_down_kernel, bs=bs, tm=tm, hb=hb, multi=n_tiles > 1)
    args = [hist] + ([up] if n_tiles > 1 else []) + [up, x1, gate2, w, b, w_down, g]
    return pl.pallas_call(
        kern,
        grid=(n_tiles,),
        in_specs=_window_specs(hist, nf, tm, hb, n_tiles) + [pl.BlockSpec((tm, d), lambda i: (i, 0)), mod_spec,
                                                              full(w), full(b), full(w_down), full(g)],
        out_specs=pl.BlockSpec((tm, d), lambda i: (i, 0)),
        out_shape=jax.ShapeDtypeStruct((r, d), F32),
        scratch_shapes=[pltpu.VMEM((hb + tm, nf), F32)],
        compiler_params=_cparams(("arbitrary",)),
        name="ffn_down",
    )(*args)


def _sample_scores_kernel(pt_ref, q_ref, w_ref, knew_ref, *refs, n_chunks, past, n_new):
    page_refs, o_ref = refs[:PPC], refs[PPC]
    c = pl.program_id(1)
    q = q_ref[...]
    w = w_ref[...]

    def head_sum(page):
        sc = _dot_nt(q, page.astype(BF16))
        val = w * jnp.maximum(sc, 0.0)
        return val.reshape(IDX_HEADS, SUBLANES, LANES).sum(axis=0)

    tq = lax.broadcasted_iota(I32, (SUBLANES, LANES), 0)
    j = lax.broadcasted_iota(I32, (SUBLANES, LANES), 1)

    @pl.when(c < n_chunks)
    def _past():
        for r in range(PPC):
            pos = (c * PPC + r) * PAGE_SIZE + j
            o_ref[:, r * LANES:(r + 1) * LANES] = jnp.where(tq < n_new, head_sum(page_refs[r][...]), -pos.astype(F32))

    @pl.when(c == n_chunks)
    def _new():
        valid = jnp.logical_and(j <= tq, j < n_new)
        o_ref[:, 0:LANES] = jnp.where(valid, head_sum(knew_ref[...]), NEG)
        o_ref[:, LANES:] = jnp.full((SUBLANES, (PPC - 1) * LANES), NEG, F32)


def _sample_scores(page_table, q_rows, w_rows, ki_new, cache_kidx, layer):
    nb, n_pages = page_table.shape
    n_chunks = n_pages // PPC
    past = n_pages * PAGE_SIZE
    cw = PPC * PAGE_SIZE

    def page_spec(r):
        return pl.BlockSpec((None, None, PAGE_SIZE, IDX_DIM),
                            lambda b, c, pt: (layer, pt[b, jnp.minimum(c * PPC + r, n_pages - 1)], 0, 0))

    grid_spec = pltpu.PrefetchScalarGridSpec(
        num_scalar_prefetch=1,
        grid=(nb, n_chunks + 1),
        in_specs=[pl.BlockSpec((None, IDX_HEADS * SUBLANES, IDX_DIM), lambda b, c, pt: (b, 0, 0)),
                  pl.BlockSpec((None, IDX_HEADS * SUBLANES, LANES), lambda b, c, pt: (b, 0, 0)),
                  pl.BlockSpec((None, PAGE_SIZE, IDX_DIM), lambda b, c, pt: (b, 0, 0))]
                 + [page_spec(r) for r in range(PPC)],
        out_specs=pl.BlockSpec((None, SUBLANES, cw), lambda b, c, pt: (b, 0, c)),
    )
    return pl.pallas_call(
        functools.partial(_sample_scores_kernel, n_chunks=n_chunks, past=past, n_new=4),
        grid_spec=grid_spec,
        out_shape=jax.ShapeDtypeStruct((nb, SUBLANES, past + cw), F32),
        compiler_params=_cparams(("arbitrary", "arbitrary")),
        name="sample_scores",
    )(page_table, q_rows, w_rows, ki_new, *([cache_kidx] * PPC))


def _sample_select_kernel(sc_ref, lim_ref, keys_ref, thr_ref, *, topk):
    rows = sc_ref.shape[0]

    def conv(s, c):
        r0 = pl.multiple_of(s * SUB, SUB)
        keys_ref[pl.ds(r0, SUB), :] = _sortable(sc_ref[pl.ds(r0, SUB), :])
        return c

    lax.fori_loop(0, rows // SUB, conv, 0)
    thr = _select_topk(keys_ref, rows // SUB, lim_ref[...], topk, 0)
    thr_ref[...] = jnp.broadcast_to(thr, thr_ref.shape)


def _sample_select(scores_t, limit, topk):
    rows, nq = scores_t.shape
    return pl.pallas_call(
        functools.partial(_sample_select_kernel, topk=topk),
        grid=(nq // LANES,),
        in_specs=[pl.BlockSpec((rows, LANES), lambda g: (0, g)), pl.BlockSpec((1, LANES), lambda g: (0, g))],
        out_specs=[pl.BlockSpec((rows, LANES), lambda g: (0, g)), pl.BlockSpec((SUBLANES, LANES), lambda g: (0, g))],
        out_shape=[jax.ShapeDtypeStruct((rows, nq), I32), jax.ShapeDtypeStruct((SUBLANES, nq), I32)],
        compiler_params=_cparams(("arbitrary",)),
        name="sample_select",
    )(scores_t, limit)


def _sample_attn_kernel(pt_ref, qbd_ref, keys_ref, thr_ref, knew_ref, vnew_ref, *refs, n_chunks):
    k_refs, v_refs = refs[:PPC], refs[PPC:2 * PPC]
    o_ref, m_ref, l_ref, acc_ref = refs[2 * PPC:]
    c = pl.program_id(1)
    rows = N_HEADS * SUBLANES

    @pl.when(c == 0)
    def _init():
        m_ref[...] = jnp.full(m_ref.shape, NEG, F32)
        l_ref[...] = jnp.zeros(l_ref.shape, F32)
        acc_ref[...] = jnp.zeros(acc_ref.shape, F32)

    qbd = qbd_ref[...]
    thr = thr_ref[...]

    def page_update(r, kpage, vpage):
        sel8 = jnp.where(keys_ref[:, r * LANES:(r + 1) * LANES] >= thr, 1, 0)
        sel = jnp.concatenate([sel8] * N_HEADS, axis=0) > 0
        sc = jnp.where(sel, _dot_nt(qbd, kpage.astype(BF16)), 2 * NEG)
        m_old = m_ref[...]
        m_new = jnp.maximum(m_old, jnp.max(sc, axis=1, keepdims=True))
        alpha = jnp.exp(m_old - m_new)
        pm = jnp.exp(sc - m_new)
        l_ref[...] = alpha * l_ref[...] + jnp.sum(pm, axis=1, keepdims=True)
        acc_ref[...] = alpha * acc_ref[...] + _dot(pm.astype(BF16), vpage.astype(BF16))
        m_ref[...] = m_new

    @pl.when(c < n_chunks)
    def _past():
        for r in range(PPC):
            page_update(r, k_refs[r][...], v_refs[r][...])

    @pl.when(c == n_chunks)
    def _new():
        page_update(0, knew_ref[...], vnew_ref[...])
        out = acc_ref[...] / l_ref[...]
        lane = lax.broadcasted_iota(I32, (SUBLANES, D_ATTN), 1)
        res = jnp.zeros((SUBLANES, D_ATTN), F32)
        for h in range(N_HEADS):
            in_head = jnp.logical_and(lane >= h * HEAD_DIM, lane < (h + 1) * HEAD_DIM)
            res = res + jnp.where(in_head, out[h * SUBLANES:(h + 1) * SUBLANES, :], 0.0)
        o_ref[...] = res


def _sample_attn(page_table, qbd, keys, thr, k_new, v_new, cache_k, cache_v, layer):
    nb, n_pages = page_table.shape
    n_chunks = n_pages // PPC
    cw = PPC * PAGE_SIZE
    rows = N_HEADS * SUBLANES

    def page_spec(r):
        return pl.BlockSpec((None, None, PAGE_SIZE, D_ATTN),
                            lambda b, c, pt: (layer, pt[b, jnp.minimum(c * PPC + r, n_pages - 1)], 0, 0))

    grid_spec = pltpu.PrefetchScalarGridSpec(
        num_scalar_prefetch=1,
        grid=(nb, n_chunks + 1),
        in_specs=[pl.BlockSpec((None, rows, D_ATTN), lambda b, c, pt: (b, 0, 0)),
                  pl.BlockSpec((None, SUBLANES, cw), lambda b, c, pt: (b, 0, c)),
                  pl.BlockSpec((None, SUBLANES, LANES), lambda b, c, pt: (b, 0, 0)),
                  pl.BlockSpec((None, PAGE_SIZE, D_ATTN), lambda b, c, pt: (b, 0, 0)),
                  pl.BlockSpec((None, PAGE_SIZE, D_ATTN), lambda b, c, pt: (b, 0, 0))]
                 + [page_spec(r) for r in range(PPC)] * 2,
        out_specs=pl.BlockSpec((None, SUBLANES, D_ATTN), lambda b, c, pt: (b, 0, 0)),
        scratch_shapes=[pltpu.VMEM((rows, 1), F32), pltpu.VMEM((rows, 1), F32), pltpu.VMEM((rows, D_ATTN), F32)],
    )
    return pl.pallas_call(
        functools.partial(_sample_attn_kernel, n_chunks=n_chunks),
        grid_spec=grid_spec,
        out_shape=jax.ShapeDtypeStruct((nb, SUBLANES, D_ATTN), F32),
        compiler_params=_cparams(("arbitrary", "arbitrary")),
        name="sample_attn",
    )(page_table, qbd, keys, thr, k_new, v_new, *([cache_k] * PPC), *([cache_v] * PPC))


def _split_mod(mod):
    return [mod[:, i * D_MODEL:(i + 1) * D_MODEL] for i in range(6)]


def _layer_prompt(x, mod, wts):
    t = x.shape[0]
    shift1, scale1, gate1, shift2, scale2, gate2 = _split_mod(mod)
    glu, q_b, k, k_b, v, v_b, qi_b, tail = _inproj(x, shift1, scale1, wts["g_pre_mix"], wts["w_main"], wts["w_tail"], 512)
    ki = tail[:, :IDX_DIM]
    wi = tail[:, IDX_DIM:IDX_DIM + IDX_HEADS]
    conv_hist = jnp.zeros((32, D_CONV), F32)
    conv_y = _conv_ln(conv_hist, glu, wts["w_dw"], wts["b_dw"], wts["ln_g"], wts["ln_b"], bs=1, tm=256, rc=32)
    attn_y = _attn_prompt(qi_b, wi, ki.astype(BF16), q_b, k_b, v_b, min(TOPK_MAX, t // 4))
    x1, up = _mix_up(x, conv_y, attn_y, gate1, shift2, scale2, wts["g_post_mix"], wts["g_pre_ffn"],
                     wts["w_out"], wts["w_up"], 512)
    ffn_hist = jnp.zeros((SUBLANES, 2 * D_FF), F32)
    y = _ffn_down(ffn_hist, up, x1, gate2, wts["w_ffn_dw"], wts["b_ffn_dw"], wts["w_down"], wts["g_post_ffn"],
                  bs=1, tm=256)
    return y, k, v, ki, glu[t - (CONV_WIDTH - 1):], up[t - (FFN_CONV_WIDTH - 1):]


def _layer_sample(x, mod, conv_state, ffn_state, cache_k, cache_v, cache_kidx, page_table, layer, wts):
    nb, n_pages = page_table.shape
    r = x.shape[0]
    nt = r // nb
    past = n_pages * PAGE_SIZE
    shift1, scale1, gate1, shift2, scale2, gate2 = [jnp.tile(m, (nt, 1)) for m in _split_mod(mod)]
    glu, q_b, k, k_b, v, v_b, qi_b, tail = _inproj(x, shift1, scale1, wts["g_pre_mix"], wts["w_main"], wts["w_tail"], r)
    ki = tail[:, :IDX_DIM]
    wi = tail[:, IDX_DIM:IDX_DIM + IDX_HEADS]
    conv_hist = conv_state.transpose(1, 0, 2).reshape((CONV_WIDTH - 1) * nb, D_CONV)
    conv_y = _conv_ln(conv_hist, glu, wts["w_dw"], wts["b_dw"], wts["ln_g"], wts["ln_b"], bs=nb, tm=r, rc=32)

    tb = lambda a: a.reshape(nt, nb, -1).transpose(1, 0, 2)
    pad_t = lambda a: jnp.pad(a, ((0, 0), (0, SUBLANES - nt)) + ((0, 0),) * (a.ndim - 2))
    qi_s = pad_t(tb(qi_b).reshape(nb, nt, IDX_HEADS, IDX_DIM)).transpose(0, 2, 1, 3)
    q_rows = qi_s.reshape(nb, IDX_HEADS * SUBLANES, IDX_DIM)
    w_s = pad_t(tb(wi)).transpose(0, 2, 1).reshape(nb, IDX_HEADS * SUBLANES, 1)
    w_rows = jnp.broadcast_to(w_s, (nb, IDX_HEADS * SUBLANES, LANES))
    pad_keys = lambda a: jnp.pad(tb(a), ((0, 0), (0, PAGE_SIZE - nt), (0, 0)))
    scores = _sample_scores(page_table, q_rows, w_rows, pad_keys(ki), cache_kidx, layer)
    total = scores.shape[2]
    topk = min(TOPK_MAX, (past + nt) // 4)
    scores_t = scores.reshape(nb * SUBLANES, total).T
    limit = jnp.tile(past + jnp.arange(SUBLANES, dtype=I32), nb).reshape(1, nb * SUBLANES)
    keys_t, thr = _sample_select(scores_t, limit, topk)
    keys = keys_t.T.reshape(nb, SUBLANES, total)
    thr_rows = jnp.broadcast_to(thr[0].reshape(nb, SUBLANES, 1), (nb, SUBLANES, LANES))

    q_s = pad_t(tb(q_b).reshape(nb, nt, N_HEADS, HEAD_DIM)).transpose(0, 2, 1, 3)
    eye = jnp.eye(N_HEADS, dtype=BF16)
    qbd = (q_s[:, :, :, None, :] * eye[None, :, None, :, None]).reshape(nb, N_HEADS * SUBLANES, D_ATTN)
    attn = _sample_attn(page_table, qbd, keys, thr_rows, pad_keys(k), pad_keys(v), cache_k, cache_v, layer)
    attn_y = attn[:, :nt].transpose(1, 0, 2).reshape(r, D_ATTN)

    x1, up = _mix_up(x, conv_y, attn_y, gate1, shift2, scale2, wts["g_post_mix"], wts["g_pre_ffn"],
                     wts["w_out"], wts["w_up"], r)
    ffn_hist = ffn_state.transpose(1, 0, 2).reshape((FFN_CONV_WIDTH - 1) * nb, 2 * D_FF)
    y = _ffn_down(ffn_hist, up, x1, gate2, wts["w_ffn_dw"], wts["b_ffn_dw"], wts["w_down"], wts["g_post_ffn"],
                  bs=nb, tm=r)
    conv_all = jnp.concatenate([conv_hist, glu], axis=0).reshape(CONV_WIDTH - 1 + nt, nb, D_CONV)
    ffn_all = jnp.concatenate([ffn_hist, up], axis=0).reshape(FFN_CONV_WIDTH - 1 + nt, nb, 2 * D_FF)
    return (y, k, v, ki, conv_all[nt:].transpose(1, 0, 2), ffn_all[nt:].transpose(1, 0, 2))


def kernel(x_prompt, x_sample, cache_k, cache_v, cache_kidx, state_conv, state_ffn, page_table, c_prompt, c_sample, w_ada, b_ada, g_pre_mix, w_in, w_dw, b_dw, ln_g, ln_b, w_out, g_post_mix, g_pre_ffn, w_up, w_ffn_dw, b_ffn_dw, w_down, g_post_ffn):
    depth = w_ada.shape[0]
    pb, seq, d = x_prompt.shape
    db, dt, _ = x_sample.shape
    n_phys = cache_k.shape[1]
    assert pb == 1 and d == D_MODEL and seq % KT == 0 and db * dt == LANES and dt <= SUBLANES
    ck = cache_k.reshape(depth, n_phys, PAGE_SIZE, D_ATTN)
    cv = cache_v.reshape(depth, n_phys, PAGE_SIZE, D_ATTN)

    xp = x_prompt[0]
    xs = x_sample.transpose(1, 0, 2).reshape(dt * db, d)
    n_c = pb + db
    c_all = jnp.pad(jnp.concatenate([c_prompt, c_sample], axis=0), ((0, (-n_c) % SUBLANES), (0, 0)))
    row = lambda a: a.reshape(1, -1)
    outs_p, outs_s = [], []
    for l in range(depth):
        mod = _ada(c_all, w_ada[l], b_ada[l])
        wts = dict(
            g_pre_mix=row(g_pre_mix[l]), w_main=w_in[l][:, :N_MAIN].astype(BF16),
            w_tail=jnp.pad(w_in[l][:, N_MAIN:], ((0, 0), (0, LANES - IDX_DIM - IDX_HEADS))).astype(BF16),
            w_dw=w_dw[l], b_dw=row(b_dw[l]), ln_g=row(ln_g[l]), ln_b=row(ln_b[l]),
            w_out=w_out[l].astype(BF16), g_post_mix=row(g_post_mix[l]), g_pre_ffn=row(g_pre_ffn[l]),
            w_up=w_up[l].astype(BF16), w_ffn_dw=w_ffn_dw[l], b_ffn_dw=row(b_ffn_dw[l]),
            w_down=w_down[l].astype(BF16), g_post_ffn=row(g_post_ffn[l]))
        xp, kp, vp, kip, cp, fp = _layer_prompt(xp, mod[:pb], wts)
        xs, k_s, v_s, ki_s, c_s, f_s = _layer_sample(xs, mod[pb:n_c], state_conv[l], state_ffn[l], ck, cv,
                                                      cache_kidx, page_table, l, wts)
        unflat = lambda a, n: a.reshape(dt, db, n).transpose(1, 0, 2)
        outs_p.append((kp.reshape(pb, seq, N_HEADS, HEAD_DIM), vp.reshape(pb, seq, N_HEADS, HEAD_DIM),
                       kip.reshape(pb, seq, IDX_DIM), cp[None], fp[None]))
        outs_s.append((unflat(k_s, D_ATTN).reshape(db, dt, N_HEADS, HEAD_DIM),
                       unflat(v_s, D_ATTN).reshape(db, dt, N_HEADS, HEAD_DIM), unflat(ki_s, IDX_DIM), c_s, f_s))
    stack = lambda outs, i: jnp.stack([o[i] for o in outs])
    y_prompt = xp[None]
    y_sample = xs.reshape(dt, db, d).transpose(1, 0, 2)
    return (y_prompt, y_sample) + tuple(stack(outs_p, i) for i in range(5)) + tuple(stack(outs_s, i) for i in range(5))
```

```python
import functools

import jax
import jax.numpy as jnp
import numpy as np
from jax import lax
from jax.experimental import pallas as pl
from jax.experimental.pallas import tpu as pltpu

F32 = jnp.float32
BF16 = jnp.bfloat16
I32 = jnp.int32

D_MODEL = 1024
D_CONV = 512
CONV_WIDTH = 31
N_HEADS = 8
HEAD_DIM = 64
D_ATTN = N_HEADS * HEAD_DIM
IDX_HEADS = 8
IDX_DIM = 64
TOPK_MAX = 256
D_FF = 2816
FFN_CONV_WIDTH = 3
PAGE_SIZE = 128
EPS = 1e-6
NEG = -1e30
INDEX_WEIGHT_SCALE = (IDX_HEADS * IDX_DIM) ** -0.5
LOG2E = 1.4426950408889634
N_MAIN = 6 * 512
INT_MIN = -(2 ** 31)

LANES = 128
SUBLANES = 8
VMEM_LIMIT = 56 * 1024 * 1024

QB = 128
SUB = 256
KT = 1024
PPC = 16
APC = 8
SLAB = 128
N_PHASE = SLAB // SUBLANES
KEEP = 12
CAND_ROWS = N_PHASE * KEEP * SUBLANES


def _cparams(sem):
    return pltpu.CompilerParams(dimension_semantics=sem, vmem_limit_bytes=VMEM_LIMIT)


def _rms(x, g):
    return x * lax.rsqrt(jnp.mean(x * x, axis=-1, keepdims=True) + EPS) * g


def _dot(a, b):
    return jnp.dot(a, b, preferred_element_type=F32)


def _dot_nt(a, b):
    return lax.dot_general(a, b, (((1,), (1,)), ((), ())), preferred_element_type=F32)


def _ada_kernel(c_ref, w_ref, b_ref, o_ref):
    c = c_ref[...]
    s = (c * jax.nn.sigmoid(c)).astype(BF16)
    o_ref[...] = _dot(s, w_ref[...].astype(BF16)) + b_ref[...]


def _ada(c_all, w_ada, b_ada):
    r, d = c_all.shape
    n = w_ada.shape[1]
    tn = 1024
    return pl.pallas_call(
        _ada_kernel,
        grid=(n // tn,),
        in_specs=[pl.BlockSpec((r, d), lambda j: (0, 0)),
                  pl.BlockSpec((d, tn), lambda j: (0, j)),
                  pl.BlockSpec((1, tn), lambda j: (0, j))],
        out_specs=pl.BlockSpec((r, tn), lambda j: (0, j)),
        out_shape=jax.ShapeDtypeStruct((r, n), F32),
        compiler_params=_cparams(("arbitrary",)),
        name="ada",
    )(c_all, w_ada, b_ada.reshape(1, n))


def _inproj_kernel(x_ref, shift_ref, scale_ref, g_ref, wm_ref, wt_ref,
                   glu_ref, q_ref, k_ref, kb_ref, v_ref, vb_ref, qi_ref, tail_ref):
    h = _rms(x_ref[...], g_ref[...]) * (1.0 + scale_ref[...]) + shift_ref[...]
    hb = h.astype(BF16)

    def proj(i):
        return _dot(hb, wm_ref[:, i * 512:(i + 1) * 512])

    ca = proj(0)
    cg = proj(1)
    glu_ref[...] = ca * jax.nn.sigmoid(cg)
    q_ref[...] = (proj(2) * (HEAD_DIM ** -0.5 * LOG2E)).astype(BF16)
    k = proj(3)
    k_ref[...] = k
    kb_ref[...] = k.astype(BF16)
    v = proj(4)
    v_ref[...] = v
    vb_ref[...] = v.astype(BF16)
    qi_ref[...] = proj(5).astype(BF16)
    lane = lax.broadcasted_iota(I32, (1, LANES), 1)
    tail_ref[...] = _dot(hb, wt_ref[...]) * jnp.where(lane >= IDX_DIM, INDEX_WEIGHT_SCALE, 1.0)


def _inproj(x, shift, scale, g, w_main, w_tail, tm):
    r, d = x.shape
    per_row = shift.shape[0] != 1
    mod_spec = pl.BlockSpec((tm, d), lambda i: (i, 0)) if per_row else pl.BlockSpec((1, d), lambda i: (0, 0))
    row = lambda n: pl.BlockSpec((tm, n), lambda i: (i, 0))
    full = lambda a: pl.BlockSpec(a.shape, lambda i: (0, 0))
    sds = lambda n, dt: jax.ShapeDtypeStruct((r, n), dt)
    return pl.pallas_call(
        _inproj_kernel,
        grid=(r // tm,),
        in_specs=[row(d), mod_spec, mod_spec, full(g), full(w_main), full(w_tail)],
        out_specs=[row(512)] * 7 + [row(LANES)],
        out_shape=[sds(512, F32), sds(512, BF16), sds(512, F32), sds(512, BF16), sds(512, F32), sds(512, BF16),
                   sds(512, BF16), sds(LANES, F32)],
        compiler_params=_cparams(("arbitrary",)),
        name="inproj",
    )(x, shift, scale, g, w_main, w_tail)


def _fill_window(win_ref, hist_ref, halo_ref, main_ref, hb, tm):
    win_ref[hb:hb + tm, :] = main_ref[...]
    if halo_ref is None:
        win_ref[0:hb, :] = hist_ref[...]
    else:
        first = pl.program_id(0) == 0

        @pl.when(first)
        def _():
            win_ref[0:hb, :] = hist_ref[...]

        @pl.when(jnp.logical_not(first))
        def _():
            win_ref[0:hb, :] = halo_ref[...]


def _window_specs(hist, main_cols, tm, hb, n_tiles):
    specs = [pl.BlockSpec(hist.shape, lambda i: (0, 0))]
    if n_tiles > 1:
        per = tm // hb
        specs.append(pl.BlockSpec((hb, main_cols), lambda i: (jnp.maximum(i * per - 1, 0), 0)))
    specs.append(pl.BlockSpec((tm, main_cols), lambda i: (i, 0)))
    return specs


def _conv_ln_kernel(*refs, taps, bs, tm, hb, rc, multi):
    if multi:
        hist_ref, halo_ref, main_ref, w_ref, b_ref, g_ref, beta_ref, o_ref, win_ref = refs
    else:
        hist_ref, main_ref, w_ref, b_ref, g_ref, beta_ref, o_ref, win_ref = refs
        halo_ref = None
    _fill_window(win_ref, hist_ref, halo_ref, main_ref, hb, tm)
    base = hb - (taps - 1) * bs
    c = o_ref.shape[1]
    for ci in range(tm // rc):
        acc = jnp.broadcast_to(b_ref[...], (rc, c))
        for j in range(taps):
            r0 = base + ci * rc + j * bs
            acc = acc + w_ref[j:j + 1, :] * win_ref[r0:r0 + rc, :]
        mu = jnp.mean(acc, axis=-1, keepdims=True)
        xc = acc - mu
        var = jnp.mean(xc * xc, axis=-1, keepdims=True)
        y = xc * lax.rsqrt(var + EPS) * g_ref[...] + beta_ref[...]
        o_ref[ci * rc:(ci + 1) * rc, :] = y * jax.nn.sigmoid(y)


def _conv_ln(hist, glu, w, b, g, beta, *, bs, tm, rc):
    r, c = glu.shape
    taps = w.shape[0]
    hb = hist.shape[0]
    n_tiles = r // tm
    vec = lambda a: pl.BlockSpec(a.shape, lambda i: (0, 0))
    kern = functools.partial(_conv_ln_kernel, taps=taps, bs=bs, tm=tm, hb=hb, rc=rc, multi=n_tiles > 1)
    args = [hist] + ([glu] if n_tiles > 1 else []) + [glu, w, b, g, beta]
    return pl.pallas_call(
        kern,
        grid=(n_tiles,),
        in_specs=_window_specs(hist, c, tm, hb, n_tiles) + [vec(w), vec(b), vec(g), vec(beta)],
        out_specs=pl.BlockSpec((tm, c), lambda i: (i, 0)),
        out_shape=jax.ShapeDtypeStruct((r, c), F32),
        scratch_shapes=[pltpu.VMEM((hb + tm, c), F32)],
        compiler_params=_cparams(("arbitrary",)),
        name="conv_ln",
    )(*args)


def _sortable(x):
    bits = pltpu.bitcast(x, I32)
    return bits ^ ((bits >> 31) & 0x7FFFFFFF)


def _select_topk(keys_ref, cand_ref, nsub, limit, topk, demote_from):
    zero8 = jnp.zeros((SUBLANES, LANES), I32)

    def count_in(ref, n, pred):
        def body(s, acc):
            r0 = pl.multiple_of(s * SUB, SUB)
            kk = ref[pl.ds(r0, SUB), :]
            hit = jnp.where(pred(kk, r0), 1, 0)
            return acc + hit.reshape(SUB // SUBLANES, SUBLANES, LANES).sum(axis=0)
        acc = lax.fori_loop(0, n, body, zero8)
        return jnp.sum(acc, axis=0, keepdims=True)

    def count(pred):
        return count_in(keys_ref, nsub, pred)

    def kth_largest(ref, n):
        ge = lambda cand: count_in(ref, n, lambda kk, r0: kk >= cand)
        tau = jnp.where(ge(jnp.zeros((1, LANES), I32)) >= topk, 0, INT_MIN).astype(I32)

        def bit_body(b, tau):
            cand = tau | jnp.left_shift(jnp.int32(1), 30 - b)
            return jnp.where(ge(cand) >= topk, cand, tau)

        return lax.fori_loop(0, 31, bit_body, tau)

    def prefiltered():
        worst = jnp.full((SUBLANES, LANES), INT_MIN, I32)
        for ph2 in range(N_PHASE // 2):
            def body(s, ls):
                ls = list(ls)
                r0 = pl.multiple_of(s * SUB, SUB)
                for slab in range(SUB // SLAB):
                    for k in range(2):
                        x = keys_ref[pl.ds(r0 + slab * SLAB + (2 * ph2 + k) * SUBLANES, SUBLANES), :]
                        for i in range(KEEP):
                            cur = ls[k * KEEP + i]
                            up = x > cur
                            ls[k * KEEP + i] = jnp.where(up, x, cur)
                            x = jnp.where(up, cur, x)
                return tuple(ls)

            init = tuple(jnp.full((SUBLANES, LANES), INT_MIN, I32) for _ in range(2 * KEEP))
            ls = lax.fori_loop(0, nsub, body, init)
            for k in range(2):
                for i in range(KEEP):
                    c0 = ((2 * ph2 + k) * KEEP + i) * SUBLANES
                    cand_ref[c0:c0 + SUBLANES, :] = ls[k * KEEP + i]
                worst = jnp.maximum(worst, ls[k * KEEP + KEEP - 1])
        tau = kth_largest(cand_ref, CAND_ROWS // SUB)
        unsafe = jnp.max(worst, axis=0, keepdims=True) >= tau
        return tau, jnp.max(jnp.where(unsafe, 1, 0))

    tau, unsafe = lax.cond(nsub * (SUB // SLAB) > KEEP, prefiltered,
                           lambda: (jnp.zeros((1, LANES), I32), jnp.int32(1)))
    tau = lax.cond(unsafe > 0, lambda: kth_largest(keys_ref, nsub), lambda: tau)
    n_ge = count(lambda kk, r0: kk >= tau)

    row_iota = lax.broadcasted_iota(I32, (SUB, LANES), 0)

    @pl.when(jnp.max(n_ge) > topk)
    def _ties():
        n_gt = count(lambda kk, r0: kk > tau)
        keep = topk - n_gt

        def pos_body(b, q):
            cand = q | jnp.left_shift(jnp.int32(1), 30 - b)
            n = count(lambda kk, r0: jnp.logical_and(kk == tau, (r0 + row_iota) < cand))
            return jnp.where(n < keep, cand, q)

        last = lax.fori_loop(0, 31, pos_body, jnp.zeros((1, LANES), I32))
        lower = jnp.where(tau == INT_MIN, INT_MIN, tau - 1)

        def demote(s, c):
            r0 = pl.multiple_of(s * SUB, SUB)
            kk = keys_ref[pl.ds(r0, SUB), :]
            drop = jnp.logical_and(kk == tau, (r0 + row_iota) > last)
            keys_ref[pl.ds(r0, SUB), :] = jnp.where(drop, lower, kk)
            return c

        lax.fori_loop(0, nsub, demote, 0)

    def causal(s, c):
        r0 = pl.multiple_of(s * SUB, SUB)
        kk = keys_ref[pl.ds(r0, SUB), :]
        keys_ref[pl.ds(r0, SUB), :] = jnp.where((r0 + row_iota) <= limit, kk, INT_MIN)
        return c

    lax.fori_loop(demote_from, nsub, causal, 0)
    return jnp.maximum(tau, INT_MIN + 1)


def _attn_prompt_kernel(qb_tab, kt_tab, qall_ref, wt_ref, ki_ref, qbd_ref, k_ref, vt_ref, o_ref,
                        keys_ref, thr_ref, m_ref, l_ref, acc_ref, bias_ref, cand_ref, *, topk):
    i = pl.program_id(0)
    qb = qb_tab[i]
    kt = kt_tab[i]
    q_end = (qb + 1) * QB

    @pl.when(kt == 0)
    def _select():
        nsub = (q_end + SUB - 1) // SUB
        qpos = qb * QB + lax.broadcasted_iota(I32, (1, LANES), 1)
        row_iota = lax.broadcasted_iota(I32, (SUB, LANES), 0)

        def scores(s, c):
            r0 = pl.multiple_of(s * SUB, SUB)
            kis = ki_ref[pl.ds(r0, SUB), :]
            acc = jnp.zeros((SUB, LANES), F32)
            for p in range(IDX_HEADS // 2):
                r = _dot_nt(kis, qall_ref[p * 2 * QB:(p + 1) * 2 * QB, :])
                acc = acc + wt_ref[2 * p:2 * p + 1, :] * jnp.maximum(r[:, :QB], 0.0)
                acc = acc + wt_ref[2 * p + 1:2 * p + 2, :] * jnp.maximum(r[:, QB:], 0.0)
            sc = jnp.where((r0 + row_iota) <= qpos, acc, NEG)
            keys_ref[pl.ds(r0, SUB), :] = _sortable(sc)
            return c

        lax.fori_loop(0, nsub, scores, 0)
        thr = _select_topk(keys_ref, cand_ref, nsub, qpos, topk, (qb * QB) // SUB)
        thr_ref[...] = jnp.broadcast_to(thr, thr_ref.shape)

        def clear(s, c):
            r0 = pl.multiple_of(s * SUB, SUB)
            keys_ref[pl.ds(r0, SUB), :] = jnp.full((SUB, LANES), INT_MIN, I32)
            return c

        lax.fori_loop(nsub, ((q_end + KT - 1) // KT) * (KT // SUB), clear, 0)
        m_ref[...] = jnp.full(m_ref.shape, NEG, F32)
        l_ref[...] = jnp.zeros(l_ref.shape, F32)
        acc_ref[...] = jnp.zeros(acc_ref.shape, F32)

    r0 = pl.multiple_of(kt * KT, KT)
    bias_ref[...] = jnp.where(keys_ref[pl.ds(r0, KT), :] >= thr_ref[0:1, :], 0.0, 2 * NEG)
    halves = [slice(j * (KT // 2), (j + 1) * (KT // 2)) for j in range(2)]
    def qk(p):
        return [_dot_nt(k_ref[hv, p * LANES:(p + 1) * LANES], qbd_ref[p]) for hv in halves]

    def softmax(h, sts):
        hh = h % 2
        scs = [st[:, hh * QB:(hh + 1) * QB] + bias_ref[hv, :] for st, hv in zip(sts, halves)]
        m_old = m_ref[h:h + 1, :]
        m_new = jnp.maximum(m_old, jnp.maximum(jnp.max(scs[0], axis=0, keepdims=True),
                                               jnp.max(scs[1], axis=0, keepdims=True)))
        alpha = jnp.exp2(m_old - m_new)
        pms = [jnp.exp2(sc - m_new) for sc in scs]
        l_ref[h:h + 1, :] = (alpha * l_ref[h:h + 1, :] + jnp.sum(pms[0], axis=0, keepdims=True)
                             + jnp.sum(pms[1], axis=0, keepdims=True))
        m_ref[h:h + 1, :] = m_new
        return alpha, [pm.astype(BF16) for pm in pms]

    def pv(h, alpha, pms):
        rows = slice(h * HEAD_DIM, (h + 1) * HEAD_DIM)
        out = _dot(vt_ref[rows, halves[0]], pms[0]) + _dot(vt_ref[rows, halves[1]], pms[1])
        acc_ref[rows, :] = alpha * acc_ref[rows, :] + out

    nxt = qk(0)
    pending = None
    for h in range(N_HEADS):
        if h % 2 == 0:
            sts = nxt
            if h + 2 < N_HEADS:
                nxt = qk(h // 2 + 1)
        cur = (h,) + softmax(h, sts)
        if pending is not None:
            pv(*pending)
        pending = cur
    pv(*pending)

    @pl.when((kt + 1) * KT >= q_end)
    def _finish():
        for h in range(N_HEADS):
            rows = slice(h * HEAD_DIM, (h + 1) * HEAD_DIM)
            acc_ref[rows, :] = acc_ref[rows, :] / l_ref[h:h + 1, :]
        o_ref[...] = acc_ref[...].T


def _attn_prompt(qi_b, wi, ki_b, q_b, k_b, v_b, topk):
    t = qi_b.shape[0]
    nb = t // QB
    qall = qi_b.reshape(nb, QB, IDX_HEADS, IDX_DIM).transpose(0, 2, 1, 3).reshape(nb, IDX_HEADS * QB, IDX_DIM)
    qh = q_b.reshape(nb, QB, N_HEADS // 2, 2, HEAD_DIM).transpose(0, 2, 3, 1, 4)
    zero = jnp.zeros_like(qh[:, :, 0])
    qbd = jnp.concatenate([jnp.concatenate([qh[:, :, 0], zero], axis=-1),
                           jnp.concatenate([zero, qh[:, :, 1]], axis=-1)], axis=2)
    wt = wi.T
    vt = v_b.T

    steps = [(b, j) for b in range(nb) for j in range(((b + 1) * QB + KT - 1) // KT)]
    qb_tab = jnp.asarray(np.array([s[0] for s in steps], np.int32))
    kt_tab = jnp.asarray(np.array([s[1] for s in steps], np.int32))
    t_pad = ((t + KT - 1) // KT) * KT

    grid_spec = pltpu.PrefetchScalarGridSpec(
        num_scalar_prefetch=2,
        grid=(len(steps),),
        in_specs=[
            pl.BlockSpec((None, IDX_HEADS * QB, IDX_DIM), lambda i, qb, kt: (qb[i], 0, 0)),
            pl.BlockSpec((IDX_HEADS, QB), lambda i, qb, kt: (0, qb[i])),
            pl.BlockSpec((t, IDX_DIM), lambda i, qb, kt: (0, 0)),
            pl.BlockSpec((None, N_HEADS // 2, 2 * QB, LANES), lambda i, qb, kt: (qb[i], 0, 0, 0)),
            pl.BlockSpec((KT, D_ATTN), lambda i, qb, kt: (kt[i], 0)),
            pl.BlockSpec((D_ATTN, KT), lambda i, qb, kt: (0, kt[i])),
        ],
        out_specs=pl.BlockSpec((QB, D_ATTN), lambda i, qb, kt: (qb[i], 0)),
        scratch_shapes=[
            pltpu.VMEM((t_pad, LANES), I32),
            pltpu.VMEM((SUBLANES, LANES), I32),
            pltpu.VMEM((N_HEADS, LANES), F32),
            pltpu.VMEM((N_HEADS, LANES), F32),
            pltpu.VMEM((D_ATTN, LANES), F32),
            pltpu.VMEM((KT, LANES), F32),
            pltpu.VMEM((CAND_ROWS, LANES), I32),
        ],
    )
    return pl.pallas_call(
        functools.partial(_attn_prompt_kernel, topk=topk),
        grid_spec=grid_spec,
        out_shape=jax.ShapeDtypeStruct((t, D_ATTN), F32),
        compiler_params=_cparams(("arbitrary",)),
        name="attn_prompt",
    )(qb_tab, kt_tab, qall, wt, ki_b, qbd, k_b, vt)


def _mix_up_kernel(x_ref, cy_ref, ay_ref, gate_ref, shift_ref, scale_ref, gpm_ref, gpf_ref, wo_ref, wu_ref,
                   x1_ref, up_ref):
    mix = _dot(cy_ref[...].astype(BF16), wo_ref[0:D_CONV, :]) + _dot(ay_ref[...].astype(BF16), wo_ref[D_CONV:, :])
    x1 = x_ref[...] + gate_ref[...] * _rms(mix, gpm_ref[...])
    x1_ref[...] = x1
    h2 = (_rms(x1, gpf_ref[...]) * (1.0 + scale_ref[...]) + shift_ref[...]).astype(BF16)
    for c in range(up_ref.shape[1] // 512):
        up_ref[:, c * 512:(c + 1) * 512] = _dot(h2, wu_ref[:, c * 512:(c + 1) * 512])


def _mix_up(x, cy, ay, gate1, shift2, scale2, g_post_mix, g_pre_ffn, w_out, w_up, tm):
    r, d = x.shape
    nf = w_up.shape[1]
    per_row = gate1.shape[0] != 1
    mod_spec = pl.BlockSpec((tm, d), lambda i: (i, 0)) if per_row else pl.BlockSpec((1, d), lambda i: (0, 0))
    row = lambda n: pl.BlockSpec((tm, n), lambda i: (i, 0))
    full = lambda a: pl.BlockSpec(a.shape, lambda i: (0, 0))
    return pl.pallas_call(
        _mix_up_kernel,
        grid=(r // tm,),
        in_specs=[row(d), row(D_CONV), row(D_ATTN), mod_spec, mod_spec, mod_spec, full(g_post_mix), full(g_pre_ffn),
                  full(w_out), full(w_up)],
        out_specs=[row(d), row(nf)],
        out_shape=[jax.ShapeDtypeStruct((r, d), F32), jax.ShapeDtypeStruct((r, nf), F32)],
        compiler_params=_cparams(("arbitrary",)),
        name="mix_up",
    )(x, cy, ay, gate1, shift2, scale2, g_post_mix, g_pre_ffn, w_out, w_up)


def _ffn_down_kernel(*refs, bs, tm, hb, multi):
    if multi:
        hist_ref, halo_ref, main_ref, x1_ref, gate_ref, w_ref, b_ref, wd_ref, g_ref, o_ref, win_ref = refs
    else:
        hist_ref, main_ref, x1_ref, gate_ref, w_ref, b_ref, wd_ref, g_ref, o_ref, win_ref = refs
        halo_ref = None
    _fill_window(win_ref, hist_ref, halo_ref, main_ref, hb, tm)
    taps = w_ref.shape[0]
    base = hb - (taps - 1) * bs
    cw = 256

    def conv(c0):
        acc = jnp.broadcast_to(b_ref[:, c0:c0 + cw], (tm, cw))
        for j in range(taps):
            acc = acc + w_ref[j:j + 1, c0:c0 + cw] * win_ref[base + j * bs:base + j * bs + tm, c0:c0 + cw]
        return acc

    f = jnp.zeros((tm, o_ref.shape[1]), F32)
    for c in range(D_FF // cw):
        a = conv(c * cw)
        g = conv(D_FF + c * cw)
        gated = (g * jax.nn.sigmoid(g) * a).astype(BF16)
        f = f + _dot(gated, wd_ref[c * cw:(c + 1) * cw, :])
    o_ref[...] = x1_ref[...] + gate_ref[...] * _rms(f, g_ref[...])


def _ffn_down(hist, up, x1, gate2, w, b, w_down, g, *, bs, tm):
    r, nf = up.shape
    d = x1.shape[1]
    hb = hist.shape[0]
    n_tiles = r // tm
    per_row = gate2.shape[0] != 1
    mod_spec = pl.BlockSpec((tm, d), lambda i: (i, 0)) if per_row else pl.BlockSpec((1, d), lambda i: (0, 0))
    full = lambda a: pl.BlockSpec(a.shape, lambda i: (0, 0))
    kern = functools.partial(_ffn_down_kernel, bs=bs, tm=tm, hb=hb, multi=n_tiles > 1)
    args = [hist] + ([up] if n_tiles > 1 else []) + [up, x1, gate2, w, b, w_down, g]
    return pl.pallas_call(
        kern,
        grid=(n_tiles,),
        in_specs=_window_specs(hist, nf, tm, hb, n_tiles) + [pl.BlockSpec((tm, d), lambda i: (i, 0)), mod_spec,
                                                              full(w), full(b), full(w_down), full(g)],
        out_specs=pl.BlockSpec((tm, d), lambda i: (i, 0)),
        out_shape=jax.ShapeDtypeStruct((r, d), F32),
        scratch_shapes=[pltpu.VMEM((hb + tm, nf), F32)],
        compiler_params=_cparams(("arbitrary",)),
        name="ffn_down",
    )(*args)


def _sample_scores_kernel(pt_ref, q_ref, w_ref, knew_ref, *refs, n_chunks, past, n_new):
    page_refs, o_ref = refs[:PPC], refs[PPC]
    c = pl.program_id(1)
    q = q_ref[...]
    w = w_ref[...]

    def head_sum(page):
        sc = _dot_nt(q, page.astype(BF16))
        val = w * jnp.maximum(sc, 0.0)
        return val.reshape(IDX_HEADS, SUBLANES, LANES).sum(axis=0)

    tq = lax.broadcasted_iota(I32, (SUBLANES, LANES), 0)
    j = lax.broadcasted_iota(I32, (SUBLANES, LANES), 1)

    @pl.when(c < n_chunks)
    def _past():
        for r in range(PPC):
            pos = (c * PPC + r) * PAGE_SIZE + j
            o_ref[:, r * LANES:(r + 1) * LANES] = jnp.where(tq < n_new, head_sum(page_refs[r][...]), -pos.astype(F32))

    @pl.when(c == n_chunks)
    def _new():
        valid = jnp.logical_and(j <= tq, j < n_new)
        o_ref[:, 0:LANES] = jnp.where(valid, head_sum(knew_ref[...]), NEG)
        o_ref[:, LANES:] = jnp.full((SUBLANES, (PPC - 1) * LANES), NEG, F32)


def _sample_scores(page_table, q_rows, w_rows, ki_new, cache_kidx, layer):
    nb, n_pages = page_table.shape
    n_chunks = n_pages // PPC
    past = n_pages * PAGE_SIZE
    cw = PPC * PAGE_SIZE

    def page_spec(r):
        return pl.BlockSpec((None, None, PAGE_SIZE, IDX_DIM),
                            lambda b, c, pt: (layer, pt[b, jnp.minimum(c * PPC + r, n_pages - 1)], 0, 0))

    grid_spec = pltpu.PrefetchScalarGridSpec(
        num_scalar_prefetch=1,
        grid=(nb, n_chunks + 1),
        in_specs=[pl.BlockSpec((None, IDX_HEADS * SUBLANES, IDX_DIM), lambda b, c, pt: (b, 0, 0)),
                  pl.BlockSpec((None, IDX_HEADS * SUBLANES, LANES), lambda b, c, pt: (b, 0, 0)),
                  pl.BlockSpec((None, PAGE_SIZE, IDX_DIM), lambda b, c, pt: (b, 0, 0))]
                 + [page_spec(r) for r in range(PPC)],
        out_specs=pl.BlockSpec((None, SUBLANES, cw), lambda b, c, pt: (b, 0, c)),
    )
    return pl.pallas_call(
        functools.partial(_sample_scores_kernel, n_chunks=n_chunks, past=past, n_new=4),
        grid_spec=grid_spec,
        out_shape=jax.ShapeDtypeStruct((nb, SUBLANES, past + cw), F32),
        compiler_params=_cparams(("arbitrary", "arbitrary")),
        name="sample_scores",
    )(page_table, q_rows, w_rows, ki_new, *([cache_kidx] * PPC))


def _sample_select_kernel(sc_ref, lim_ref, keys_ref, thr_ref, cand_ref, *, topk):
    rows = sc_ref.shape[0]

    def conv(s, c):
        r0 = pl.multiple_of(s * SUB, SUB)
        keys_ref[pl.ds(r0, SUB), :] = _sortable(sc_ref[pl.ds(r0, SUB), :])
        return c

    lax.fori_loop(0, rows // SUB, conv, 0)
    thr = _select_topk(keys_ref, cand_ref, rows // SUB, lim_ref[...], topk, 0)
    thr_ref[...] = jnp.broadcast_to(thr, thr_ref.shape)


def _sample_select(scores_t, limit, topk):
    rows, nq = scores_t.shape
    return pl.pallas_call(
        functools.partial(_sample_select_kernel, topk=topk),
        grid=(nq // LANES,),
        in_specs=[pl.BlockSpec((rows, LANES), lambda g: (0, g)), pl.BlockSpec((1, LANES), lambda g: (0, g))],
        out_specs=[pl.BlockSpec((rows, LANES), lambda g: (0, g)), pl.BlockSpec((SUBLANES, LANES), lambda g: (0, g))],
        out_shape=[jax.ShapeDtypeStruct((rows, nq), I32), jax.ShapeDtypeStruct((SUBLANES, nq), I32)],
        scratch_shapes=[pltpu.VMEM((CAND_ROWS, LANES), I32)],
        compiler_params=_cparams(("arbitrary",)),
        name="sample_select",
    )(scores_t, limit)


def _sample_attn_kernel(pt_ref, q_ref, keys_ref, thr_ref, knew_ref, vnew_ref, rexp_ref, *refs, n_chunks):
    k_refs, v_refs = refs[:APC], refs[APC:2 * APC]
    o_ref, m_ref, l_ref, acc_ref, hb_ref = refs[2 * APC:]
    c = pl.program_id(1)
    rows = N_HEADS * SUBLANES
    cols = PAGE_SIZE * N_HEADS

    @pl.when(c == 0)
    def _init():
        m_ref[...] = jnp.full(m_ref.shape, NEG, F32)
        l_ref[...] = jnp.zeros(l_ref.shape, F32)
        acc_ref[...] = jnp.zeros(acc_ref.shape, F32)
        row_head = lax.broadcasted_iota(I32, (rows, cols), 0) // SUBLANES
        col_head = lax.broadcasted_iota(I32, (rows, cols), 1) % N_HEADS
        hb_ref[...] = jnp.where(row_head == col_head, 0.0, 2 * NEG)

    q = q_ref[...]
    thr = thr_ref[...]

    def pages_update(pages):
        scs = []
        for r, (kp, _) in enumerate(pages):
            kf = kp[...].reshape(cols, HEAD_DIM).astype(BF16)
            sel8 = jnp.where(keys_ref[:, r * LANES:(r + 1) * LANES] >= thr, 1.0, 0.0).astype(BF16)
            selx = _dot(sel8, rexp_ref[...])
            sbias = jnp.where(selx > 0.5, 0.0, 2 * NEG)
            scs.append(_dot_nt(q, kf) + (jnp.concatenate([sbias] * N_HEADS, axis=0) + hb_ref[...]))
        m_old = m_ref[...]
        m_new = m_old
        for sc in scs:
            m_new = jnp.maximum(m_new, jnp.max(sc, axis=1, keepdims=True))
        alpha = jnp.exp2(m_old - m_new)
        l_new = alpha * l_ref[...]
        acc = alpha * acc_ref[...]
        for sc, (_, vp) in zip(scs, pages):
            pm = jnp.exp2(sc - m_new)
            l_new = l_new + jnp.sum(pm, axis=1, keepdims=True)
            acc = acc + _dot(pm.astype(BF16), vp[...].reshape(cols, HEAD_DIM).astype(BF16))
        l_ref[...] = l_new
        acc_ref[...] = acc
        m_ref[...] = m_new

    @pl.when(c < n_chunks)
    def _past():
        pages_update(list(zip(k_refs, v_refs)))

    @pl.when(c == n_chunks)
    def _new():
        pages_update([(knew_ref, vnew_ref)])
        out = acc_ref[...] / l_ref[...]
        o_ref[...] = jnp.concatenate([out[h * SUBLANES:(h + 1) * SUBLANES, :] for h in range(N_HEADS)], axis=1)


def _sample_attn(page_table, q_rows, keys, thr, k_new, v_new, cache_k, cache_v, layer):
    nb, n_pages = page_table.shape
    n_chunks = n_pages // APC
    cw = APC * PAGE_SIZE
    rows = N_HEADS * SUBLANES
    cols = PAGE_SIZE * N_HEADS
    rexp = jnp.repeat(jnp.eye(PAGE_SIZE, dtype=BF16), N_HEADS, axis=1)

    def page_spec(r):
        return pl.BlockSpec((None, None, PAGE_SIZE, N_HEADS, HEAD_DIM),
                            lambda b, c, pt: (layer, pt[b, jnp.minimum(c * APC + r, n_pages - 1)], 0, 0, 0))

    new_spec = pl.BlockSpec((None, PAGE_SIZE, N_HEADS, HEAD_DIM), lambda b, c, pt: (b, 0, 0, 0))
    grid_spec = pltpu.PrefetchScalarGridSpec(
        num_scalar_prefetch=1,
        grid=(nb, n_chunks + 1),
        in_specs=[pl.BlockSpec((None, rows, HEAD_DIM), lambda b, c, pt: (b, 0, 0)),
                  pl.BlockSpec((None, SUBLANES, cw), lambda b, c, pt: (b, 0, c)),
                  pl.BlockSpec((None, SUBLANES, LANES), lambda b, c, pt: (b, 0, 0)),
                  new_spec, new_spec,
                  pl.BlockSpec((PAGE_SIZE, cols), lambda b, c, pt: (0, 0))]
                 + [page_spec(r) for r in range(APC)] * 2,
        out_specs=pl.BlockSpec((None, SUBLANES, D_ATTN), lambda b, c, pt: (b, 0, 0)),
        scratch_shapes=[pltpu.VMEM((rows, 1), F32), pltpu.VMEM((rows, 1), F32), pltpu.VMEM((rows, HEAD_DIM), F32),
                        pltpu.VMEM((rows, cols), F32)],
    )
    return pl.pallas_call(
        functools.partial(_sample_attn_kernel, n_chunks=n_chunks),
        grid_spec=grid_spec,
        out_shape=jax.ShapeDtypeStruct((nb, SUBLANES, D_ATTN), F32),
        compiler_params=_cparams(("arbitrary", "arbitrary")),
        name="sample_attn",
    )(page_table, q_rows, keys, thr, k_new, v_new, rexp, *([cache_k] * APC), *([cache_v] * APC))


def _split_mod(mod):
    return [mod[:, i * D_MODEL:(i + 1) * D_MODEL] for i in range(6)]


def _layer_prompt(x, mod, wts):
    t = x.shape[0]
    shift1, scale1, gate1, shift2, scale2, gate2 = _split_mod(mod)
    glu, q_b, k, k_b, v, v_b, qi_b, tail = _inproj(x, shift1, scale1, wts["g_pre_mix"], wts["w_main"], wts["w_tail"], 512)
    ki = tail[:, :IDX_DIM]
    wi = tail[:, IDX_DIM:IDX_DIM + IDX_HEADS]
    conv_hist = jnp.zeros((32, D_CONV), F32)
    conv_y = _conv_ln(conv_hist, glu, wts["w_dw"], wts["b_dw"], wts["ln_g"], wts["ln_b"], bs=1, tm=256, rc=32)
    attn_y = _attn_prompt(qi_b, wi, ki.astype(BF16), q_b, k_b, v_b, min(TOPK_MAX, t // 4))
    x1, up = _mix_up(x, conv_y, attn_y, gate1, shift2, scale2, wts["g_post_mix"], wts["g_pre_ffn"],
                     wts["w_out"], wts["w_up"], 512)
    ffn_hist = jnp.zeros((SUBLANES, 2 * D_FF), F32)
    y = _ffn_down(ffn_hist, up, x1, gate2, wts["w_ffn_dw"], wts["b_ffn_dw"], wts["w_down"], wts["g_post_ffn"],
                  bs=1, tm=256)
    return y, k, v, ki, glu[t - (CONV_WIDTH - 1):], up[t - (FFN_CONV_WIDTH - 1):]


def _layer_sample(x, mod, conv_state, ffn_state, cache_k, cache_v, cache_kidx, page_table, layer, wts):
    nb, n_pages = page_table.shape
    r = x.shape[0]
    nt = r // nb
    past = n_pages * PAGE_SIZE
    shift1, scale1, gate1, shift2, scale2, gate2 = [jnp.tile(m, (nt, 1)) for m in _split_mod(mod)]
    glu, q_b, k, k_b, v, v_b, qi_b, tail = _inproj(x, shift1, scale1, wts["g_pre_mix"], wts["w_main"], wts["w_tail"], r)
    ki = tail[:, :IDX_DIM]
    wi = tail[:, IDX_DIM:IDX_DIM + IDX_HEADS]
    conv_hist = conv_state.transpose(1, 0, 2).reshape((CONV_WIDTH - 1) * nb, D_CONV)
    conv_y = _conv_ln(conv_hist, glu, wts["w_dw"], wts["b_dw"], wts["ln_g"], wts["ln_b"], bs=nb, tm=r, rc=32)

    tb = lambda a: a.reshape(nt, nb, -1).transpose(1, 0, 2)
    pad_t = lambda a: jnp.pad(a, ((0, 0), (0, SUBLANES - nt)) + ((0, 0),) * (a.ndim - 2))
    qi_s = pad_t(tb(qi_b).reshape(nb, nt, IDX_HEADS, IDX_DIM)).transpose(0, 2, 1, 3)
    q_rows = qi_s.reshape(nb, IDX_HEADS * SUBLANES, IDX_DIM)
    w_s = pad_t(tb(wi)).transpose(0, 2, 1).reshape(nb, IDX_HEADS * SUBLANES, 1)
    w_rows = jnp.broadcast_to(w_s, (nb, IDX_HEADS * SUBLANES, LANES))
    pad_keys = lambda a: jnp.pad(tb(a), ((0, 0), (0, PAGE_SIZE - nt), (0, 0)))
    scores = _sample_scores(page_table, q_rows, w_rows, pad_keys(ki), cache_kidx, layer)
    total = scores.shape[2]
    topk = min(TOPK_MAX, (past + nt) // 4)
    scores_t = scores.reshape(nb * SUBLANES, total).T
    limit = jnp.tile(past + jnp.arange(SUBLANES, dtype=I32), nb).reshape(1, nb * SUBLANES)
    keys_t, thr = _sample_select(scores_t, limit, topk)
    keys = keys_t.T.reshape(nb, SUBLANES, total)
    thr_rows = jnp.broadcast_to(thr[0].reshape(nb, SUBLANES, 1), (nb, SUBLANES, LANES))

    q_s = pad_t(tb(q_b).reshape(nb, nt, N_HEADS, HEAD_DIM)).transpose(0, 2, 1, 3)
    new_rows = lambda a: pad_keys(a).reshape(nb, PAGE_SIZE, N_HEADS, HEAD_DIM)
    attn = _sample_attn(page_table, q_s.reshape(nb, N_HEADS * SUBLANES, HEAD_DIM), keys, thr_rows,
                        new_rows(k), new_rows(v), cache_k, cache_v, layer)
    attn_y = attn[:, :nt].transpose(1, 0, 2).reshape(r, D_ATTN)

    x1, up = _mix_up(x, conv_y, attn_y, gate1, shift2, scale2, wts["g_post_mix"], wts["g_pre_ffn"],
                     wts["w_out"], wts["w_up"], r)
    ffn_hist = ffn_state.transpose(1, 0, 2).reshape((FFN_CONV_WIDTH - 1) * nb, 2 * D_FF)
    y = _ffn_down(ffn_hist, up, x1, gate2, wts["w_ffn_dw"], wts["b_ffn_dw"], wts["w_down"], wts["g_post_ffn"],
                  bs=nb, tm=r)
    conv_all = jnp.concatenate([conv_hist, glu], axis=0).reshape(CONV_WIDTH - 1 + nt, nb, D_CONV)
    ffn_all = jnp.concatenate([ffn_hist, up], axis=0).reshape(FFN_CONV_WIDTH - 1 + nt, nb, 2 * D_FF)
    return (y, k, v, ki, conv_all[nt:].transpose(1, 0, 2), ffn_all[nt:].transpose(1, 0, 2))


def kernel(x_prompt, x_sample, cache_k, cache_v, cache_kidx, state_conv, state_ffn, page_table, c_prompt, c_sample, w_ada, b_ada, g_pre_mix, w_in, w_dw, b_dw, ln_g, ln_b, w_out, g_post_mix, g_pre_ffn, w_up, w_ffn_dw, b_ffn_dw, w_down, g_post_ffn):
    depth = w_ada.shape[0]
    pb, seq, d = x_prompt.shape
    db, dt, _ = x_sample.shape
    n_phys = cache_k.shape[1]
    assert pb == 1 and d == D_MODEL and seq % KT == 0 and db * dt == LANES and dt <= SUBLANES

    xp = x_prompt[0]
    xs = x_sample.transpose(1, 0, 2).reshape(dt * db, d)
    n_c = pb + db
    c_all = jnp.pad(jnp.concatenate([c_prompt, c_sample], axis=0), ((0, (-n_c) % SUBLANES), (0, 0)))
    row = lambda a: a.reshape(1, -1)
    outs_p, outs_s = [], []
    for l in range(depth):
        mod = _ada(c_all, w_ada[l], b_ada[l])
        wts = dict(
            g_pre_mix=row(g_pre_mix[l]), w_main=w_in[l][:, :N_MAIN].astype(BF16),
            w_tail=jnp.pad(w_in[l][:, N_MAIN:], ((0, 0), (0, LANES - IDX_DIM - IDX_HEADS))).astype(BF16),
            w_dw=w_dw[l], b_dw=row(b_dw[l]), ln_g=row(ln_g[l]), ln_b=row(ln_b[l]),
            w_out=w_out[l].astype(BF16), g_post_mix=row(g_post_mix[l]), g_pre_ffn=row(g_pre_ffn[l]),
            w_up=w_up[l].astype(BF16), w_ffn_dw=w_ffn_dw[l], b_ffn_dw=row(b_ffn_dw[l]),
            w_down=w_down[l].astype(BF16), g_post_ffn=row(g_post_ffn[l]))
        xp, kp, vp, kip, cp, fp = _layer_prompt(xp, mod[:pb], wts)
        xs, k_s, v_s, ki_s, c_s, f_s = _layer_sample(xs, mod[pb:n_c], state_conv[l], state_ffn[l], cache_k, cache_v,
                                                      cache_kidx, page_table, l, wts)
        unflat = lambda a, n: a.reshape(dt, db, n).transpose(1, 0, 2)
        outs_p.append((kp.reshape(pb, seq, N_HEADS, HEAD_DIM), vp.reshape(pb, seq, N_HEADS, HEAD_DIM),
                       kip.reshape(pb, seq, IDX_DIM), cp[None], fp[None]))
        outs_s.append((unflat(k_s, D_ATTN).reshape(db, dt, N_HEADS, HEAD_DIM),
                       unflat(v_s, D_ATTN).reshape(db, dt, N_HEADS, HEAD_DIM), unflat(ki_s, IDX_DIM), c_s, f_s))
    stack = lambda outs, i: jnp.stack([o[i] for o in outs])
    y_prompt = xp[None]
    y_sample = xs.reshape(dt, db, d).transpose(1, 0, 2)
    return (y_prompt, y_sample) + tuple(stack(outs_p, i) for i in range(5)) + tuple(stack(outs_s, i) for i in range(5))
```

```python
import functools

import jax
import jax.numpy as jnp
import numpy as np
from jax import lax
from jax.experimental import pallas as pl
from jax.experimental.pallas import tpu as pltpu

F32 = jnp.float32
BF16 = jnp.bfloat16
I32 = jnp.int32

D_MODEL = 1024
D_CONV = 512
CONV_WIDTH = 31
N_HEADS = 8
HEAD_DIM = 64
D_ATTN = N_HEADS * HEAD_DIM
IDX_HEADS = 8
IDX_DIM = 64
TOPK_MAX = 256
D_FF = 2816
FFN_CONV_WIDTH = 3
PAGE_SIZE = 128
EPS = 1e-6
NEG = -1e30
INDEX_WEIGHT_SCALE = (IDX_HEADS * IDX_DIM) ** -0.5
LOG2E = 1.4426950408889634
N_MAIN = 6 * 512
INT_MIN = -(2 ** 31)

LANES = 128
SUBLANES = 8
VMEM_LIMIT = 56 * 1024 * 1024

QB = 128
SUB = 256
KT = 1024
PPC = 16
APC = 8
SLAB = 128
N_PHASE = SLAB // SUBLANES
KEEP = 12
CAND_ROWS = N_PHASE * KEEP * SUBLANES


def _cparams(sem):
    return pltpu.CompilerParams(dimension_semantics=sem, vmem_limit_bytes=VMEM_LIMIT)


def _rms(x, g):
    return x * lax.rsqrt(jnp.mean(x * x, axis=-1, keepdims=True) + EPS) * g


def _dot(a, b):
    return jnp.dot(a, b, preferred_element_type=F32)


def _dot_nt(a, b):
    return lax.dot_general(a, b, (((1,), (1,)), ((), ())), preferred_element_type=F32)


def _ada_kernel(c_ref, w_ref, b_ref, o_ref):
    c = c_ref[...]
    s = (c * jax.nn.sigmoid(c)).astype(BF16)
    o_ref[...] = _dot(s, w_ref[...].astype(BF16)) + b_ref[...]


def _ada(c_all, w_ada, b_ada):
    r, d = c_all.shape
    n = w_ada.shape[1]
    tn = 1024
    return pl.pallas_call(
        _ada_kernel,
        grid=(n // tn,),
        in_specs=[pl.BlockSpec((r, d), lambda j: (0, 0)),
                  pl.BlockSpec((d, tn), lambda j: (0, j)),
                  pl.BlockSpec((1, tn), lambda j: (0, j))],
        out_specs=pl.BlockSpec((r, tn), lambda j: (0, j)),
        out_shape=jax.ShapeDtypeStruct((r, n), F32),
        compiler_params=_cparams(("arbitrary",)),
        name="ada",
    )(c_all, w_ada, b_ada.reshape(1, n))


def _inproj_kernel(x_ref, shift_ref, scale_ref, g_ref, wm_ref, wt_ref,
                   glu_ref, q_ref, k_ref, kb_ref, v_ref, vb_ref, qi_ref, tail_ref):
    h = _rms(x_ref[...], g_ref[...]) * (1.0 + scale_ref[...]) + shift_ref[...]
    hb = h.astype(BF16)

    def proj(i):
        return _dot(hb, wm_ref[:, i * 512:(i + 1) * 512])

    ca = proj(0)
    cg = proj(1)
    glu_ref[...] = ca * jax.nn.sigmoid(cg)
    q_ref[...] = (proj(2) * (HEAD_DIM ** -0.5 * LOG2E)).astype(BF16)
    k = proj(3)
    k_ref[...] = k
    kb_ref[...] = k.astype(BF16)
    v = proj(4)
    v_ref[...] = v
    vb_ref[...] = v.astype(BF16)
    qi_ref[...] = proj(5).astype(BF16)
    lane = lax.broadcasted_iota(I32, (1, LANES), 1)
    tail_ref[...] = _dot(hb, wt_ref[...]) * jnp.where(lane >= IDX_DIM, INDEX_WEIGHT_SCALE, 1.0)


def _inproj(x, shift, scale, g, w_main, w_tail, tm):
    r, d = x.shape
    per_row = shift.shape[0] != 1
    mod_spec = pl.BlockSpec((tm, d), lambda i: (i, 0)) if per_row else pl.BlockSpec((1, d), lambda i: (0, 0))
    row = lambda n: pl.BlockSpec((tm, n), lambda i: (i, 0))
    full = lambda a: pl.BlockSpec(a.shape, lambda i: (0, 0))
    sds = lambda n, dt: jax.ShapeDtypeStruct((r, n), dt)
    return pl.pallas_call(
        _inproj_kernel,
        grid=(r // tm,),
        in_specs=[row(d), mod_spec, mod_spec, full(g), full(w_main), full(w_tail)],
        out_specs=[row(512)] * 7 + [row(LANES)],
        out_shape=[sds(512, F32), sds(512, BF16), sds(512, F32), sds(512, BF16), sds(512, F32), sds(512, BF16),
                   sds(512, BF16), sds(LANES, F32)],
        compiler_params=_cparams(("arbitrary",)),
        name="inproj",
    )(x, shift, scale, g, w_main, w_tail)


def _fill_window(win_ref, hist_ref, halo_ref, main_ref, hb, tm):
    win_ref[hb:hb + tm, :] = main_ref[...]
    if halo_ref is None:
        win_ref[0:hb, :] = hist_ref[...]
    else:
        first = pl.program_id(0) == 0

        @pl.when(first)
        def _():
            win_ref[0:hb, :] = hist_ref[...]

        @pl.when(jnp.logical_not(first))
        def _():
            win_ref[0:hb, :] = halo_ref[...]


def _window_specs(hist, main_cols, tm, hb, n_tiles):
    specs = [pl.BlockSpec(hist.shape, lambda i: (0, 0))]
    if n_tiles > 1:
        per = tm // hb
        specs.append(pl.BlockSpec((hb, main_cols), lambda i: (jnp.maximum(i * per - 1, 0), 0)))
    specs.append(pl.BlockSpec((tm, main_cols), lambda i: (i, 0)))
    return specs


def _conv_ln_kernel(*refs, taps, bs, tm, hb, rc, multi):
    if multi:
        hist_ref, halo_ref, main_ref, w_ref, b_ref, g_ref, beta_ref, o_ref, win_ref = refs
    else:
        hist_ref, main_ref, w_ref, b_ref, g_ref, beta_ref, o_ref, win_ref = refs
        halo_ref = None
    _fill_window(win_ref, hist_ref, halo_ref, main_ref, hb, tm)
    base = hb - (taps - 1) * bs
    c = o_ref.shape[1]
    for ci in range(tm // rc):
        acc = jnp.broadcast_to(b_ref[...], (rc, c))
        for j in range(taps):
            r0 = base + ci * rc + j * bs
            acc = acc + w_ref[j:j + 1, :] * win_ref[r0:r0 + rc, :]
        mu = jnp.mean(acc, axis=-1, keepdims=True)
        xc = acc - mu
        var = jnp.mean(xc * xc, axis=-1, keepdims=True)
        y = xc * lax.rsqrt(var + EPS) * g_ref[...] + beta_ref[...]
        o_ref[ci * rc:(ci + 1) * rc, :] = y * jax.nn.sigmoid(y)


def _conv_ln(hist, glu, w, b, g, beta, *, bs, tm, rc):
    r, c = glu.shape
    taps = w.shape[0]
    hb = hist.shape[0]
    n_tiles = r // tm
    vec = lambda a: pl.BlockSpec(a.shape, lambda i: (0, 0))
    kern = functools.partial(_conv_ln_kernel, taps=taps, bs=bs, tm=tm, hb=hb, rc=rc, multi=n_tiles > 1)
    args = [hist] + ([glu] if n_tiles > 1 else []) + [glu, w, b, g, beta]
    return pl.pallas_call(
        kern,
        grid=(n_tiles,),
        in_specs=_window_specs(hist, c, tm, hb, n_tiles) + [vec(w), vec(b), vec(g), vec(beta)],
        out_specs=pl.BlockSpec((tm, c), lambda i: (i, 0)),
        out_shape=jax.ShapeDtypeStruct((r, c), F32),
        scratch_shapes=[pltpu.VMEM((hb + tm, c), F32)],
        compiler_params=_cparams(("arbitrary",)),
        name="conv_ln",
    )(*args)


def _sortable(x):
    bits = pltpu.bitcast(x, I32)
    return bits ^ ((bits >> 31) & 0x7FFFFFFF)


def _select_topk(keys_ref, vals_ref, cand_ref, nsub, limit, topk, demote_from):
    zero8 = jnp.zeros((SUBLANES, LANES), I32)

    def count_in(ref, n, pred):
        def body(s, acc):
            r0 = pl.multiple_of(s * SUB, SUB)
            kk = ref[pl.ds(r0, SUB), :]
            hit = jnp.where(pred(kk, r0), 1, 0)
            return acc + hit.reshape(SUB // SUBLANES, SUBLANES, LANES).sum(axis=0)
        acc = lax.fori_loop(0, n, body, zero8)
        return jnp.sum(acc, axis=0, keepdims=True)

    def count(pred):
        return count_in(keys_ref, nsub, pred)

    def kth_largest(ref, n):
        ge = lambda cand: count_in(ref, n, lambda kk, r0: kk >= cand)
        tau = jnp.where(ge(jnp.zeros((1, LANES), I32)) >= topk, 0, INT_MIN).astype(I32)

        def bit_body(b, tau):
            cand = tau | jnp.left_shift(jnp.int32(1), 30 - b)
            return jnp.where(ge(cand) >= topk, cand, tau)

        return lax.fori_loop(0, 31, bit_body, tau)

    def prefiltered():
        src_ref = keys_ref if vals_ref is None else vals_ref
        lowest = INT_MIN if vals_ref is None else -jnp.inf
        to_key = (lambda v: v) if vals_ref is None else _sortable
        worst = jnp.full((SUBLANES, LANES), INT_MIN, I32)
        for ph2 in range(N_PHASE // 2):
            def body(s, ls):
                ls = list(ls)
                r0 = pl.multiple_of(s * SUB, SUB)
                for slab in range(SUB // SLAB):
                    for k in range(2):
                        x = src_ref[pl.ds(r0 + slab * SLAB + (2 * ph2 + k) * SUBLANES, SUBLANES), :]
                        for i in range(KEEP):
                            cur = ls[k * KEEP + i]
                            if vals_ref is None:
                                up = x > cur
                                ls[k * KEEP + i] = jnp.where(up, x, cur)
                                x = jnp.where(up, cur, x)
                            else:
                                ls[k * KEEP + i] = jnp.maximum(cur, x)
                                x = jnp.minimum(cur, x)
                return tuple(ls)

            init = tuple(jnp.full((SUBLANES, LANES), lowest, src_ref.dtype) for _ in range(2 * KEEP))
            ls = [to_key(v) for v in lax.fori_loop(0, nsub, body, init)]
            for k in range(2):
                for i in range(KEEP):
                    c0 = ((2 * ph2 + k) * KEEP + i) * SUBLANES
                    cand_ref[c0:c0 + SUBLANES, :] = ls[k * KEEP + i]
                worst = jnp.maximum(worst, ls[k * KEEP + KEEP - 1])
        tau = kth_largest(cand_ref, CAND_ROWS // SUB)
        unsafe = jnp.max(worst, axis=0, keepdims=True) >= tau
        return tau, jnp.max(jnp.where(unsafe, 1, 0))

    tau, unsafe = lax.cond(nsub * (SUB // SLAB) > KEEP, prefiltered,
                           lambda: (jnp.zeros((1, LANES), I32), jnp.int32(1)))
    tau = lax.cond(unsafe > 0, lambda: kth_largest(keys_ref, nsub), lambda: tau)
    n_ge = count(lambda kk, r0: kk >= tau)

    row_iota = lax.broadcasted_iota(I32, (SUB, LANES), 0)

    @pl.when(jnp.max(n_ge) > topk)
    def _ties():
        n_gt = count(lambda kk, r0: kk > tau)
        keep = topk - n_gt

        def pos_body(b, q):
            cand = q | jnp.left_shift(jnp.int32(1), 30 - b)
            n = count(lambda kk, r0: jnp.logical_and(kk == tau, (r0 + row_iota) < cand))
            return jnp.where(n < keep, cand, q)

        last = lax.fori_loop(0, 31, pos_body, jnp.zeros((1, LANES), I32))
        lower = jnp.where(tau == INT_MIN, INT_MIN, tau - 1)

        def demote(s, c):
            r0 = pl.multiple_of(s * SUB, SUB)
            kk = keys_ref[pl.ds(r0, SUB), :]
            drop = jnp.logical_and(kk == tau, (r0 + row_iota) > last)
            keys_ref[pl.ds(r0, SUB), :] = jnp.where(drop, lower, kk)
            return c

        lax.fori_loop(0, nsub, demote, 0)

    def causal(s, c):
        r0 = pl.multiple_of(s * SUB, SUB)
        kk = keys_ref[pl.ds(r0, SUB), :]
        keys_ref[pl.ds(r0, SUB), :] = jnp.where((r0 + row_iota) <= limit, kk, INT_MIN)
        return c

    lax.fori_loop(demote_from, nsub, causal, 0)
    return jnp.maximum(tau, INT_MIN + 1)


def _attn_prompt_kernel(qb_tab, kt_tab, qall_ref, wt_ref, ki_ref, qbd_ref, k_ref, vt_ref, o_ref,
                        keys_ref, thr_ref, m_ref, l_ref, acc_ref, bias_ref, cand_ref, dots_ref, vals_ref, *, topk):
    i = pl.program_id(0)
    qb = qb_tab[i]
    kt = kt_tab[i]
    q_end = (qb + 1) * QB

    @pl.when(kt == 0)
    def _select():
        nsub = (q_end + SUB - 1) // SUB
        qpos = qb * QB + lax.broadcasted_iota(I32, (1, LANES), 1)
        row_iota = lax.broadcasted_iota(I32, (SUB, LANES), 0)

        def head_dots(s, slot):
            s = jnp.minimum(s, nsub - 1)
            for half in range(2):
                off = half * (SUB // 2)
                kis = ki_ref[pl.ds(pl.multiple_of(s * SUB + off, SUB // 2), SUB // 2), :]
                for p in range(IDX_HEADS // 2):
                    cols = slice(p * 2 * QB, (p + 1) * 2 * QB)
                    dots_ref[slot * SUB + off:slot * SUB + off + SUB // 2, cols] = _dot_nt(kis, qall_ref[cols, :])

        def head_sum(s, slot):
            r0 = pl.multiple_of(jnp.minimum(s, nsub - 1) * SUB, SUB)
            acc = jnp.zeros((SUB, LANES), F32)
            for h in range(IDX_HEADS):
                acc = acc + wt_ref[h:h + 1, :] * jnp.maximum(dots_ref[slot * SUB:(slot + 1) * SUB, h * QB:(h + 1) * QB], 0.0)
            acc = jnp.where(acc == 0.0, 0.0, acc)
            sc = jnp.where((r0 + row_iota) <= qpos, acc, NEG)
            vals_ref[pl.ds(r0, SUB), :] = sc
            keys_ref[pl.ds(r0, SUB), :] = _sortable(sc)

        def scores(j, c):
            head_dots(2 * j + 1, 1)
            head_sum(2 * j, 0)
            head_dots(2 * j + 2, 0)
            head_sum(2 * j + 1, 1)
            return c

        head_dots(0, 0)
        lax.fori_loop(0, (nsub + 1) // 2, scores, 0)
        thr = _select_topk(keys_ref, vals_ref, cand_ref, nsub, qpos, topk, (qb * QB) // SUB)
        thr_ref[...] = jnp.broadcast_to(thr, thr_ref.shape)

        def clear(s, c):
            r0 = pl.multiple_of(s * SUB, SUB)
            keys_ref[pl.ds(r0, SUB), :] = jnp.full((SUB, LANES), INT_MIN, I32)
            return c

        lax.fori_loop(nsub, ((q_end + KT - 1) // KT) * (KT // SUB), clear, 0)
        m_ref[...] = jnp.full(m_ref.shape, NEG, F32)
        l_ref[...] = jnp.zeros(l_ref.shape, F32)
        acc_ref[...] = jnp.zeros(acc_ref.shape, F32)

    r0 = pl.multiple_of(kt * KT, KT)
    bias_ref[...] = jnp.where(keys_ref[pl.ds(r0, KT), :] >= thr_ref[0:1, :], 0.0, 2 * NEG)
    halves = [slice(j * (KT // 2), (j + 1) * (KT // 2)) for j in range(2)]
    def qk(p):
        return [_dot_nt(k_ref[hv, p * LANES:(p + 1) * LANES], qbd_ref[p]) for hv in halves]

    def softmax(h, sts):
        hh = h % 2
        scs = [st[:, hh * QB:(hh + 1) * QB] + bias_ref[hv, :] for st, hv in zip(sts, halves)]
        m_old = m_ref[h:h + 1, :]
        m_new = jnp.maximum(m_old, jnp.maximum(jnp.max(scs[0], axis=0, keepdims=True),
                                               jnp.max(scs[1], axis=0, keepdims=True)))
        alpha = jnp.exp2(m_old - m_new)
        pms = [jnp.exp2(sc - m_new) for sc in scs]
        l_ref[h:h + 1, :] = (alpha * l_ref[h:h + 1, :] + jnp.sum(pms[0], axis=0, keepdims=True)
                             + jnp.sum(pms[1], axis=0, keepdims=True))
        m_ref[h:h + 1, :] = m_new
        return alpha, [pm.astype(BF16) for pm in pms]

    def pv(h, alpha, pms):
        rows = slice(h * HEAD_DIM, (h + 1) * HEAD_DIM)
        out = _dot(vt_ref[rows, halves[0]], pms[0]) + _dot(vt_ref[rows, halves[1]], pms[1])
        acc_ref[rows, :] = alpha * acc_ref[rows, :] + out

    nxt = qk(0)
    pending = None
    for h in range(N_HEADS):
        if h % 2 == 0:
            sts = nxt
            if h + 2 < N_HEADS:
                nxt = qk(h // 2 + 1)
        cur = (h,) + softmax(h, sts)
        if pending is not None:
            pv(*pending)
        pending = cur
    pv(*pending)

    @pl.when((kt + 1) * KT >= q_end)
    def _finish():
        for h in range(N_HEADS):
            rows = slice(h * HEAD_DIM, (h + 1) * HEAD_DIM)
            acc_ref[rows, :] = acc_ref[rows, :] / l_ref[h:h + 1, :]
        o_ref[...] = acc_ref[...].T


def _attn_prompt(qi_b, wi, ki_b, q_b, k_b, v_b, topk):
    t = qi_b.shape[0]
    nb = t // QB
    qall = qi_b.reshape(nb, QB, IDX_HEADS, IDX_DIM).transpose(0, 2, 1, 3).reshape(nb, IDX_HEADS * QB, IDX_DIM)
    qh = q_b.reshape(nb, QB, N_HEADS // 2, 2, HEAD_DIM).transpose(0, 2, 3, 1, 4)
    zero = jnp.zeros_like(qh[:, :, 0])
    qbd = jnp.concatenate([jnp.concatenate([qh[:, :, 0], zero], axis=-1),
                           jnp.concatenate([zero, qh[:, :, 1]], axis=-1)], axis=2)
    wt = wi.T
    vt = v_b.T

    steps = [(b, j) for b in range(nb) for j in range(((b + 1) * QB + KT - 1) // KT)]
    qb_tab = jnp.asarray(np.array([s[0] for s in steps], np.int32))
    kt_tab = jnp.asarray(np.array([s[1] for s in steps], np.int32))
    t_pad = ((t + KT - 1) // KT) * KT

    grid_spec = pltpu.PrefetchScalarGridSpec(
        num_scalar_prefetch=2,
        grid=(len(steps),),
        in_specs=[
            pl.BlockSpec((None, IDX_HEADS * QB, IDX_DIM), lambda i, qb, kt: (qb[i], 0, 0)),
            pl.BlockSpec((IDX_HEADS, QB), lambda i, qb, kt: (0, qb[i])),
            pl.BlockSpec((t, IDX_DIM), lambda i, qb, kt: (0, 0)),
            pl.BlockSpec((None, N_HEADS // 2, 2 * QB, LANES), lambda i, qb, kt: (qb[i], 0, 0, 0)),
            pl.BlockSpec((KT, D_ATTN), lambda i, qb, kt: (kt[i], 0)),
            pl.BlockSpec((D_ATTN, KT), lambda i, qb, kt: (0, kt[i])),
        ],
        out_specs=pl.BlockSpec((QB, D_ATTN), lambda i, qb, kt: (qb[i], 0)),
        scratch_shapes=[
            pltpu.VMEM((t_pad, LANES), I32),
            pltpu.VMEM((SUBLANES, LANES), I32),
            pltpu.VMEM((N_HEADS, LANES), F32),
            pltpu.VMEM((N_HEADS, LANES), F32),
            pltpu.VMEM((D_ATTN, LANES), F32),
            pltpu.VMEM((KT, LANES), F32),
            pltpu.VMEM((CAND_ROWS, LANES), I32),
            pltpu.VMEM((2 * SUB, IDX_HEADS * QB), F32),
            pltpu.VMEM((t_pad, LANES), F32),
        ],
    )
    return pl.pallas_call(
        functools.partial(_attn_prompt_kernel, topk=topk),
        grid_spec=grid_spec,
        out_shape=jax.ShapeDtypeStruct((t, D_ATTN), F32),
        compiler_params=_cparams(("arbitrary",)),
        name="attn_prompt",
    )(qb_tab, kt_tab, qall, wt, ki_b, qbd, k_b, vt)


def _mix_up_kernel(x_ref, cy_ref, ay_ref, gate_ref, shift_ref, scale_ref, gpm_ref, gpf_ref, wo_ref, wu_ref,
                   x1_ref, up_ref):
    mix = _dot(cy_ref[...].astype(BF16), wo_ref[0:D_CONV, :]) + _dot(ay_ref[...].astype(BF16), wo_ref[D_CONV:, :])
    x1 = x_ref[...] + gate_ref[...] * _rms(mix, gpm_ref[...])
    x1_ref[...] = x1
    h2 = (_rms(x1, gpf_ref[...]) * (1.0 + scale_ref[...]) + shift_ref[...]).astype(BF16)
    for c in range(up_ref.shape[1] // 512):
        up_ref[:, c * 512:(c + 1) * 512] = _dot(h2, wu_ref[:, c * 512:(c + 1) * 512])


def _mix_up(x, cy, ay, gate1, shift2, scale2, g_post_mix, g_pre_ffn, w_out, w_up, tm):
    r, d = x.shape
    nf = w_up.shape[1]
    per_row = gate1.shape[0] != 1
    mod_spec = pl.BlockSpec((tm, d), lambda i: (i, 0)) if per_row else pl.BlockSpec((1, d), lambda i: (0, 0))
    row = lambda n: pl.BlockSpec((tm, n), lambda i: (i, 0))
    full = lambda a: pl.BlockSpec(a.shape, lambda i: (0, 0))
    return pl.pallas_call(
        _mix_up_kernel,
        grid=(r // tm,),
        in_specs=[row(d), row(D_CONV), row(D_ATTN), mod_spec, mod_spec, mod_spec, full(g_post_mix), full(g_pre_ffn),
                  full(w_out), full(w_up)],
        out_specs=[row(d), row(nf)],
        out_shape=[jax.ShapeDtypeStruct((r, d), F32), jax.ShapeDtypeStruct((r, nf), F32)],
        compiler_params=_cparams(("arbitrary",)),
        name="mix_up",
    )(x, cy, ay, gate1, shift2, scale2, g_post_mix, g_pre_ffn, w_out, w_up)


def _ffn_down_kernel(*refs, bs, tm, hb, multi):
    if multi:
        hist_ref, halo_ref, main_ref, x1_ref, gate_ref, w_ref, b_ref, wd_ref, g_ref, o_ref, win_ref = refs
    else:
        hist_ref, main_ref, x1_ref, gate_ref, w_ref, b_ref, wd_ref, g_ref, o_ref, win_ref = refs
        halo_ref = None
    _fill_window(win_ref, hist_ref, halo_ref, main_ref, hb, tm)
    taps = w_ref.shape[0]
    base = hb - (taps - 1) * bs
    cw = 256

    def conv(c0):
        acc = jnp.broadcast_to(b_ref[:, c0:c0 + cw], (tm, cw))
        for j in range(taps):
            acc = acc + w_ref[j:j + 1, c0:c0 + cw] * win_ref[base + j * bs:base + j * bs + tm, c0:c0 + cw]
        return acc

    f = jnp.zeros((tm, o_ref.shape[1]), F32)
    for c in range(D_FF // cw):
        a = conv(c * cw)
        g = conv(D_FF + c * cw)
        gated = (g * jax.nn.sigmoid(g) * a).astype(BF16)
        f = f + _dot(gated, wd_ref[c * cw:(c + 1) * cw, :])
    o_ref[...] = x1_ref[...] + gate_ref[...] * _rms(f, g_ref[...])


def _ffn_down(hist, up, x1, gate2, w, b, w_down, g, *, bs, tm):
    r, nf = up.shape
    d = x1.shape[1]
    hb = hist.shape[0]
    n_tiles = r // tm
    per_row = gate2.shape[0] != 1
    mod_spec = pl.BlockSpec((tm, d), lambda i: (i, 0)) if per_row else pl.BlockSpec((1, d), lambda i: (0, 0))
    full = lambda a: pl.BlockSpec(a.shape, lambda i: (0, 0))
    kern = functools.partial(_ffn_down_kernel, bs=bs, tm=tm, hb=hb, multi=n_tiles > 1)
    args = [hist] + ([up] if n_tiles > 1 else []) + [up, x1, gate2, w, b, w_down, g]
    return pl.pallas_call(
        kern,
        grid=(n_tiles,),
        in_specs=_window_specs(hist, nf, tm, hb, n_tiles) + [pl.BlockSpec((tm, d), lambda i: (i, 0)), mod_spec,
                                                              full(w), full(b), full(w_down), full(g)],
        out_specs=pl.BlockSpec((tm, d), lambda i: (i, 0)),
        out_shape=jax.ShapeDtypeStruct((r, d), F32),
        scratch_shapes=[pltpu.VMEM((hb + tm, nf), F32)],
        compiler_params=_cparams(("arbitrary",)),
        name="ffn_down",
    )(*args)


def _sample_scores_kernel(pt_ref, q_ref, w_ref, knew_ref, *refs, n_chunks, past, n_new):
    page_refs, o_ref = refs[:PPC], refs[PPC]
    c = pl.program_id(1)
    q = q_ref[...]
    w = w_ref[...]

    def head_sum(page):
        sc = _dot(q, page.astype(BF16))
        val = w * jnp.maximum(sc, 0.0)
        tot = val.reshape(IDX_HEADS, SUBLANES, LANES).sum(axis=0)
        return jnp.where(tot == 0.0, 0.0, tot)

    tq = lax.broadcasted_iota(I32, (SUBLANES, LANES), 0)
    j = lax.broadcasted_iota(I32, (SUBLANES, LANES), 1)

    @pl.when(c < n_chunks)
    def _past():
        for r in range(PPC):
            pos = (c * PPC + r) * PAGE_SIZE + j
            o_ref[:, r * LANES:(r + 1) * LANES] = jnp.where(tq < n_new, head_sum(page_refs[r][...]), -pos.astype(F32))

    @pl.when(c == n_chunks)
    def _new():
        valid = jnp.logical_and(j <= tq, j < n_new)
        o_ref[:, 0:LANES] = jnp.where(valid, head_sum(knew_ref[...]), NEG)
        o_ref[:, LANES:] = jnp.full((SUBLANES, (PPC - 1) * LANES), NEG, F32)


def _sample_scores(page_table, q_rows, w_rows, kit_new, cache_kit, layer):
    nb, n_pages = page_table.shape
    n_chunks = n_pages // PPC
    past = n_pages * PAGE_SIZE
    cw = PPC * PAGE_SIZE

    def page_spec(r):
        return pl.BlockSpec((None, None, IDX_DIM, PAGE_SIZE),
                            lambda b, c, pt: (layer, pt[b, jnp.minimum(c * PPC + r, n_pages - 1)], 0, 0))

    grid_spec = pltpu.PrefetchScalarGridSpec(
        num_scalar_prefetch=1,
        grid=(nb, n_chunks + 1),
        in_specs=[pl.BlockSpec((None, IDX_HEADS * SUBLANES, IDX_DIM), lambda b, c, pt: (b, 0, 0)),
                  pl.BlockSpec((None, IDX_HEADS * SUBLANES, LANES), lambda b, c, pt: (b, 0, 0)),
                  pl.BlockSpec((None, IDX_DIM, PAGE_SIZE), lambda b, c, pt: (b, 0, 0))]
                 + [page_spec(r) for r in range(PPC)],
        out_specs=pl.BlockSpec((None, SUBLANES, cw), lambda b, c, pt: (b, 0, c)),
    )
    return pl.pallas_call(
        functools.partial(_sample_scores_kernel, n_chunks=n_chunks, past=past, n_new=4),
        grid_spec=grid_spec,
        out_shape=jax.ShapeDtypeStruct((nb, SUBLANES, past + cw), F32),
        compiler_params=_cparams(("arbitrary", "arbitrary")),
        name="sample_scores",
    )(page_table, q_rows, w_rows, kit_new, *([cache_kit] * PPC))


def _sample_select_kernel(sc_ref, lim_ref, keys_ref, thr_ref, cand_ref, *, topk):
    rows = sc_ref.shape[0]

    def conv(s, c):
        r0 = pl.multiple_of(s * SUB, SUB)
        keys_ref[pl.ds(r0, SUB), :] = _sortable(sc_ref[pl.ds(r0, SUB), :])
        return c

    lax.fori_loop(0, rows // SUB, conv, 0)
    thr = _select_topk(keys_ref, None, cand_ref, rows // SUB, lim_ref[...], topk, 0)
    thr_ref[...] = jnp.broadcast_to(thr, thr_ref.shape)


def _sample_select(scores_t, limit, topk):
    rows, nq = scores_t.shape
    return pl.pallas_call(
        functools.partial(_sample_select_kernel, topk=topk),
        grid=(nq // LANES,),
        in_specs=[pl.BlockSpec((rows, LANES), lambda g: (0, g)), pl.BlockSpec((1, LANES), lambda g: (0, g))],
        out_specs=[pl.BlockSpec((rows, LANES), lambda g: (0, g)), pl.BlockSpec((SUBLANES, LANES), lambda g: (0, g))],
        out_shape=[jax.ShapeDtypeStruct((rows, nq), I32), jax.ShapeDtypeStruct((SUBLANES, nq), I32)],
        scratch_shapes=[pltpu.VMEM((CAND_ROWS, LANES), I32)],
        compiler_params=_cparams(("arbitrary",)),
        name="sample_select",
    )(scores_t, limit)


def _sample_attn_kernel(pt_ref, q_ref, keys_ref, thr_ref, knew_ref, vnew_ref, *refs, n_chunks):
    k_refs, v_refs = refs[:APC], refs[APC:2 * APC]
    o_ref, m_ref, l_ref, acc_ref = refs[2 * APC:]
    c = pl.program_id(1)

    @pl.when(c == 0)
    def _init():
        m_ref[...] = jnp.full(m_ref.shape, NEG, F32)
        l_ref[...] = jnp.zeros(l_ref.shape, F32)
        acc_ref[...] = jnp.zeros(acc_ref.shape, F32)

    thr = thr_ref[...]

    def pages_update(pages):
        n = len(pages)
        bias = jnp.where(keys_ref[:, 0:n * LANES] >= thr[:, 0:1], 0.0, 2 * NEG)
        scs = []
        for h in range(N_HEADS):
            kt = jnp.concatenate([kp[h].astype(BF16) for kp, _ in pages], axis=1)
            scs.append(_dot(q_ref[h * SUBLANES:(h + 1) * SUBLANES, :], kt) + bias)
        sc = jnp.concatenate(scs, axis=0)
        m_old = m_ref[...]
        m_new = jnp.maximum(m_old, jnp.max(sc, axis=1, keepdims=True))
        alpha = jnp.exp2(m_old - m_new)
        pm = jnp.exp2(sc - m_new)
        l_ref[...] = alpha * l_ref[...] + jnp.sum(pm, axis=1, keepdims=True)
        pb = pm.astype(BF16)
        outs = []
        for h in range(N_HEADS):
            vt = jnp.concatenate([vp[h].astype(BF16) for _, vp in pages], axis=1)
            outs.append(_dot_nt(pb[h * SUBLANES:(h + 1) * SUBLANES, :], vt))
        acc_ref[...] = alpha * acc_ref[...] + jnp.concatenate(outs, axis=0)
        m_ref[...] = m_new

    @pl.when(c < n_chunks)
    def _past():
        pages_update(list(zip(k_refs, v_refs)))

    @pl.when(c == n_chunks)
    def _new():
        pages_update([(knew_ref, vnew_ref)])
        out = acc_ref[...] / l_ref[...]
        o_ref[...] = jnp.concatenate([out[h * SUBLANES:(h + 1) * SUBLANES, :] for h in range(N_HEADS)], axis=1)


def _sample_attn(page_table, q_rows, keys, thr, kt_new, vt_new, cache_kt, cache_vt, layer):
    nb, n_pages = page_table.shape
    n_chunks = n_pages // APC
    cw = APC * PAGE_SIZE
    rows = N_HEADS * SUBLANES

    def page_spec(r):
        return pl.BlockSpec((None, None, N_HEADS, HEAD_DIM, PAGE_SIZE),
                            lambda b, c, pt: (layer, pt[b, jnp.minimum(c * APC + r, n_pages - 1)], 0, 0, 0))

    new_spec = pl.BlockSpec((None, N_HEADS, HEAD_DIM, PAGE_SIZE), lambda b, c, pt: (b, 0, 0, 0))
    grid_spec = pltpu.PrefetchScalarGridSpec(
        num_scalar_prefetch=1,
        grid=(nb, n_chunks + 1),
        in_specs=[pl.BlockSpec((None, rows, HEAD_DIM), lambda b, c, pt: (b, 0, 0)),
                  pl.BlockSpec((None, SUBLANES, cw), lambda b, c, pt: (b, 0, c)),
                  pl.BlockSpec((None, SUBLANES, LANES), lambda b, c, pt: (b, 0, 0)),
                  new_spec, new_spec]
                 + [page_spec(r) for r in range(APC)] * 2,
        out_specs=pl.BlockSpec((None, SUBLANES, D_ATTN), lambda b, c, pt: (b, 0, 0)),
        scratch_shapes=[pltpu.VMEM((rows, 1), F32), pltpu.VMEM((rows, 1), F32), pltpu.VMEM((rows, HEAD_DIM), F32)],
    )
    return pl.pallas_call(
        functools.partial(_sample_attn_kernel, n_chunks=n_chunks),
        grid_spec=grid_spec,
        out_shape=jax.ShapeDtypeStruct((nb, SUBLANES, D_ATTN), F32),
        compiler_params=_cparams(("arbitrary", "arbitrary")),
        name="sample_attn",
    )(page_table, q_rows, keys, thr, kt_new, vt_new, *([cache_kt] * APC), *([cache_vt] * APC))


def _split_mod(mod):
    return [mod[:, i * D_MODEL:(i + 1) * D_MODEL] for i in range(6)]


def _layer_prompt(x, mod, wts):
    t = x.shape[0]
    shift1, scale1, gate1, shift2, scale2, gate2 = _split_mod(mod)
    glu, q_b, k, k_b, v, v_b, qi_b, tail = _inproj(x, shift1, scale1, wts["g_pre_mix"], wts["w_main"], wts["w_tail"], 512)
    ki = tail[:, :IDX_DIM]
    wi = tail[:, IDX_DIM:IDX_DIM + IDX_HEADS]
    conv_hist = jnp.zeros((32, D_CONV), F32)
    conv_y = _conv_ln(conv_hist, glu, wts["w_dw"], wts["b_dw"], wts["ln_g"], wts["ln_b"], bs=1, tm=256, rc=32)
    attn_y = _attn_prompt(qi_b, wi, ki.astype(BF16), q_b, k_b, v_b, min(TOPK_MAX, t // 4))
    x1, up = _mix_up(x, conv_y, attn_y, gate1, shift2, scale2, wts["g_post_mix"], wts["g_pre_ffn"],
                     wts["w_out"], wts["w_up"], 512)
    ffn_hist = jnp.zeros((SUBLANES, 2 * D_FF), F32)
    y = _ffn_down(ffn_hist, up, x1, gate2, wts["w_ffn_dw"], wts["b_ffn_dw"], wts["w_down"], wts["g_post_ffn"],
                  bs=1, tm=256)
    return y, k, v, ki, glu[t - (CONV_WIDTH - 1):], up[t - (FFN_CONV_WIDTH - 1):]


def _layer_sample(x, mod, conv_state, ffn_state, cache_kt, cache_vt, cache_kit, page_table, layer, wts):
    nb, n_pages = page_table.shape
    r = x.shape[0]
    nt = r // nb
    past = n_pages * PAGE_SIZE
    shift1, scale1, gate1, shift2, scale2, gate2 = [jnp.tile(m, (nt, 1)) for m in _split_mod(mod)]
    glu, q_b, k, k_b, v, v_b, qi_b, tail = _inproj(x, shift1, scale1, wts["g_pre_mix"], wts["w_main"], wts["w_tail"], r)
    ki = tail[:, :IDX_DIM]
    wi = tail[:, IDX_DIM:IDX_DIM + IDX_HEADS]
    conv_hist = conv_state.transpose(1, 0, 2).reshape((CONV_WIDTH - 1) * nb, D_CONV)
    conv_y = _conv_ln(conv_hist, glu, wts["w_dw"], wts["b_dw"], wts["ln_g"], wts["ln_b"], bs=nb, tm=r, rc=32)

    tb = lambda a: a.reshape(nt, nb, -1).transpose(1, 0, 2)
    pad_t = lambda a: jnp.pad(a, ((0, 0), (0, SUBLANES - nt)) + ((0, 0),) * (a.ndim - 2))
    qi_s = pad_t(tb(qi_b).reshape(nb, nt, IDX_HEADS, IDX_DIM)).transpose(0, 2, 1, 3)
    q_rows = qi_s.reshape(nb, IDX_HEADS * SUBLANES, IDX_DIM)
    w_s = pad_t(tb(wi)).transpose(0, 2, 1).reshape(nb, IDX_HEADS * SUBLANES, 1)
    w_rows = jnp.broadcast_to(w_s, (nb, IDX_HEADS * SUBLANES, LANES))
    pad_keys = lambda a: jnp.pad(tb(a), ((0, 0), (0, PAGE_SIZE - nt), (0, 0)))
    kit_new = pad_keys(ki).transpose(0, 2, 1)
    scores = _sample_scores(page_table, q_rows, w_rows, kit_new, cache_kit, layer)
    total = scores.shape[2]
    topk = min(TOPK_MAX, (past + nt) // 4)
    scores_t = scores.reshape(nb * SUBLANES, total).T
    limit = jnp.tile(past + jnp.arange(SUBLANES, dtype=I32), nb).reshape(1, nb * SUBLANES)
    keys_t, thr = _sample_select(scores_t, limit, topk)
    keys = keys_t.T.reshape(nb, SUBLANES, total)
    thr_rows = jnp.broadcast_to(thr[0].reshape(nb, SUBLANES, 1), (nb, SUBLANES, LANES))

    q_s = pad_t(tb(q_b).reshape(nb, nt, N_HEADS, HEAD_DIM)).transpose(0, 2, 1, 3)
    new_t = lambda a: pad_keys(a).reshape(nb, PAGE_SIZE, N_HEADS, HEAD_DIM).transpose(0, 2, 3, 1)
    attn = _sample_attn(page_table, q_s.reshape(nb, N_HEADS * SUBLANES, HEAD_DIM), keys, thr_rows,
                        new_t(k), new_t(v), cache_kt, cache_vt, layer)
    attn_y = attn[:, :nt].transpose(1, 0, 2).reshape(r, D_ATTN)

    x1, up = _mix_up(x, conv_y, attn_y, gate1, shift2, scale2, wts["g_post_mix"], wts["g_pre_ffn"],
                     wts["w_out"], wts["w_up"], r)
    ffn_hist = ffn_state.transpose(1, 0, 2).reshape((FFN_CONV_WIDTH - 1) * nb, 2 * D_FF)
    y = _ffn_down(ffn_hist, up, x1, gate2, wts["w_ffn_dw"], wts["b_ffn_dw"], wts["w_down"], wts["g_post_ffn"],
                  bs=nb, tm=r)
    conv_all = jnp.concatenate([conv_hist, glu], axis=0).reshape(CONV_WIDTH - 1 + nt, nb, D_CONV)
    ffn_all = jnp.concatenate([ffn_hist, up], axis=0).reshape(FFN_CONV_WIDTH - 1 + nt, nb, 2 * D_FF)
    return (y, k, v, ki, conv_all[nt:].transpose(1, 0, 2), ffn_all[nt:].transpose(1, 0, 2))


def kernel(x_prompt, x_sample, cache_k, cache_v, cache_kidx, state_conv, state_ffn, page_table, c_prompt, c_sample, w_ada, b_ada, g_pre_mix, w_in, w_dw, b_dw, ln_g, ln_b, w_out, g_post_mix, g_pre_ffn, w_up, w_ffn_dw, b_ffn_dw, w_down, g_post_ffn):
    depth = w_ada.shape[0]
    pb, seq, d = x_prompt.shape
    db, dt, _ = x_sample.shape
    assert pb == 1 and d == D_MODEL and seq % KT == 0 and db * dt == LANES and dt <= SUBLANES
    cache_kt = cache_k.transpose(0, 1, 3, 4, 2)
    cache_vt = cache_v.transpose(0, 1, 3, 4, 2)
    cache_kit = cache_kidx.transpose(0, 1, 3, 2)

    xp = x_prompt[0]
    xs = x_sample.transpose(1, 0, 2).reshape(dt * db, d)
    n_c = pb + db
    c_all = jnp.pad(jnp.concatenate([c_prompt, c_sample], axis=0), ((0, (-n_c) % SUBLANES), (0, 0)))
    row = lambda a: a.reshape(1, -1)
    outs_p, outs_s = [], []
    for l in range(depth):
        mod = _ada(c_all, w_ada[l], b_ada[l])
        wts = dict(
            g_pre_mix=row(g_pre_mix[l]), w_main=w_in[l][:, :N_MAIN].astype(BF16),
            w_tail=jnp.pad(w_in[l][:, N_MAIN:], ((0, 0), (0, LANES - IDX_DIM - IDX_HEADS))).astype(BF16),
            w_dw=w_dw[l], b_dw=row(b_dw[l]), ln_g=row(ln_g[l]), ln_b=row(ln_b[l]),
            w_out=w_out[l].astype(BF16), g_post_mix=row(g_post_mix[l]), g_pre_ffn=row(g_pre_ffn[l]),
            w_up=w_up[l].astype(BF16), w_ffn_dw=w_ffn_dw[l], b_ffn_dw=row(b_ffn_dw[l]),
            w_down=w_down[l].astype(BF16), g_post_ffn=row(g_post_ffn[l]))
        xp, kp, vp, kip, cp, fp = _layer_prompt(xp, mod[:pb], wts)
        xs, k_s, v_s, ki_s, c_s, f_s = _layer_sample(xs, mod[pb:n_c], state_conv[l], state_ffn[l], cache_kt, cache_vt,
                                                      cache_kit, page_table, l, wts)
        unflat = lambda a, n: a.reshape(dt, db, n).transpose(1, 0, 2)
        outs_p.append((kp.reshape(pb, seq, N_HEADS, HEAD_DIM), vp.reshape(pb, seq, N_HEADS, HEAD_DIM),
                       kip.reshape(pb, seq, IDX_DIM), cp[None], fp[None]))
        outs_s.append((unflat(k_s, D_ATTN).reshape(db, dt, N_HEADS, HEAD_DIM),
                       unflat(v_s, D_ATTN).reshape(db, dt, N_HEADS, HEAD_DIM), unflat(ki_s, IDX_DIM), c_s, f_s))
    stack = lambda outs, i: jnp.stack([o[i] for o in outs])
    y_prompt = xp[None]
    y_sample = xs.reshape(dt, db, d).transpose(1, 0, 2)
    return (y_prompt, y_sample) + tuple(stack(outs_p, i) for i in range(5)) + tuple(stack(outs_s, i) for i in range(5))
```

```python
import functools

import jax
import jax.numpy as jnp
import numpy as np
from jax import lax
from jax.experimental import pallas as pl
from jax.experimental.pallas import tpu as pltpu

F32 = jnp.float32
BF16 = jnp.bfloat16
I32 = jnp.int32

D_MODEL = 1024
D_CONV = 512
CONV_WIDTH = 31
N_HEADS = 8
HEAD_DIM = 64
D_ATTN = N_HEADS * HEAD_DIM
IDX_HEADS = 8
IDX_DIM = 64
TOPK_MAX = 256
D_FF = 2816
FFN_CONV_WIDTH = 3
PAGE_SIZE = 128
EPS = 1e-6
NEG = -1e30
INDEX_WEIGHT_SCALE = (IDX_HEADS * IDX_DIM) ** -0.5
LOG2E = 1.4426950408889634
N_MAIN = 6 * 512
INT_MIN = -(2 ** 31)

LANES = 128
SUBLANES = 8
VMEM_LIMIT = 56 * 1024 * 1024

QB = 128
SUB = 256
HT = 1024
KT = 2 * HT
PPC = 16
APC = 8
SLAB = 128
N_PHASE = SLAB // SUBLANES
KEEP = 12
CAND_ROWS = N_PHASE * KEEP * SUBLANES


def _cparams(sem):
    return pltpu.CompilerParams(dimension_semantics=sem, vmem_limit_bytes=VMEM_LIMIT)


def _rms(x, g):
    return x * lax.rsqrt(jnp.mean(x * x, axis=-1, keepdims=True) + EPS) * g


def _dot(a, b):
    return jnp.dot(a, b, preferred_element_type=F32)


def _dot_nt(a, b):
    return lax.dot_general(a, b, (((1,), (1,)), ((), ())), preferred_element_type=F32)


def _ada_kernel(c_ref, w_ref, b_ref, o_ref):
    c = c_ref[...]
    s = (c * jax.nn.sigmoid(c)).astype(BF16)
    o_ref[...] = _dot(s, w_ref[...].astype(BF16)) + b_ref[...]


def _ada(c_all, w_ada, b_ada):
    r, d = c_all.shape
    n = w_ada.shape[1]
    tn = 1024
    return pl.pallas_call(
        _ada_kernel,
        grid=(n // tn,),
        in_specs=[pl.BlockSpec((r, d), lambda j: (0, 0)),
                  pl.BlockSpec((d, tn), lambda j: (0, j)),
                  pl.BlockSpec((1, tn), lambda j: (0, j))],
        out_specs=pl.BlockSpec((r, tn), lambda j: (0, j)),
        out_shape=jax.ShapeDtypeStruct((r, n), F32),
        compiler_params=_cparams(("arbitrary",)),
        name="ada",
    )(c_all, w_ada, b_ada.reshape(1, n))


def _inproj_kernel(x_ref, shift_ref, scale_ref, g_ref, wm_ref, wt_ref,
                   glu_ref, q_ref, k_ref, kb_ref, v_ref, vb_ref, qi_ref, tail_ref):
    h = _rms(x_ref[...], g_ref[...]) * (1.0 + scale_ref[...]) + shift_ref[...]
    hb = h.astype(BF16)

    def proj(i):
        return _dot(hb, wm_ref[:, i * 512:(i + 1) * 512])

    ca = proj(0)
    cg = proj(1)
    glu_ref[...] = ca * jax.nn.sigmoid(cg)
    q_ref[...] = (proj(2) * (HEAD_DIM ** -0.5 * LOG2E)).astype(BF16)
    k = proj(3)
    k_ref[...] = k
    kb_ref[...] = k.astype(BF16)
    v = proj(4)
    v_ref[...] = v
    vb_ref[...] = v.astype(BF16)
    qi_ref[...] = proj(5).astype(BF16)
    lane = lax.broadcasted_iota(I32, (1, LANES), 1)
    tail_ref[...] = _dot(hb, wt_ref[...]) * jnp.where(lane >= IDX_DIM, INDEX_WEIGHT_SCALE, 1.0)


def _inproj(x, shift, scale, g, w_main, w_tail, tm):
    r, d = x.shape
    per_row = shift.shape[0] != 1
    mod_spec = pl.BlockSpec((tm, d), lambda i: (i, 0)) if per_row else pl.BlockSpec((1, d), lambda i: (0, 0))
    row = lambda n: pl.BlockSpec((tm, n), lambda i: (i, 0))
    full = lambda a: pl.BlockSpec(a.shape, lambda i: (0, 0))
    sds = lambda n, dt: jax.ShapeDtypeStruct((r, n), dt)
    return pl.pallas_call(
        _inproj_kernel,
        grid=(r // tm,),
        in_specs=[row(d), mod_spec, mod_spec, full(g), full(w_main), full(w_tail)],
        out_specs=[row(512)] * 7 + [row(LANES)],
        out_shape=[sds(512, F32), sds(512, BF16), sds(512, F32), sds(512, BF16), sds(512, F32), sds(512, BF16),
                   sds(512, BF16), sds(LANES, F32)],
        compiler_params=_cparams(("arbitrary",)),
        name="inproj",
    )(x, shift, scale, g, w_main, w_tail)


def _fill_window(win_ref, hist_ref, halo_ref, main_ref, hb, tm):
    win_ref[hb:hb + tm, :] = main_ref[...]
    if halo_ref is None:
        win_ref[0:hb, :] = hist_ref[...]
    else:
        first = pl.program_id(0) == 0

        @pl.when(first)
        def _():
            win_ref[0:hb, :] = hist_ref[...]

        @pl.when(jnp.logical_not(first))
        def _():
            win_ref[0:hb, :] = halo_ref[...]


def _window_specs(hist, main_cols, tm, hb, n_tiles):
    specs = [pl.BlockSpec(hist.shape, lambda i: (0, 0))]
    if n_tiles > 1:
        per = tm // hb
        specs.append(pl.BlockSpec((hb, main_cols), lambda i: (jnp.maximum(i * per - 1, 0), 0)))
    specs.append(pl.BlockSpec((tm, main_cols), lambda i: (i, 0)))
    return specs


def _tap_residues(taps, bs, hb):
    base = hb - (taps - 1) * bs
    return sorted({(base + j * bs) % SUBLANES for j in range(taps)} - {0})


def _conv_ln_kernel(*refs, taps, bs, tm, hb, rc, multi):
    residues = _tap_residues(taps, bs, hb)
    refs = list(refs)
    sh_ref = refs.pop() if residues else None
    if multi:
        hist_ref, halo_ref, main_ref, w_ref, b_ref, g_ref, beta_ref, o_ref, win_ref = refs
    else:
        hist_ref, main_ref, w_ref, b_ref, g_ref, beta_ref, o_ref, win_ref = refs
        halo_ref = None
    _fill_window(win_ref, hist_ref, halo_ref, main_ref, hb, tm)
    base = hb - (taps - 1) * bs
    c = o_ref.shape[1]
    span = hb + tm - SUBLANES
    for rho in residues:
        sh_ref[rho, 0:span, :] = win_ref[rho:rho + span, :]

    def rows_at(r0):
        rho = r0 % SUBLANES
        return win_ref[r0:r0 + rc, :] if rho == 0 else sh_ref[rho, r0 - rho:r0 - rho + rc, :]

    for ci in range(tm // rc):
        acc = jnp.broadcast_to(b_ref[...], (rc, c))
        for j in range(taps):
            acc = acc + w_ref[j:j + 1, :] * rows_at(base + ci * rc + j * bs)
        mu = jnp.mean(acc, axis=-1, keepdims=True)
        xc = acc - mu
        var = jnp.mean(xc * xc, axis=-1, keepdims=True)
        y = xc * lax.rsqrt(var + EPS) * g_ref[...] + beta_ref[...]
        o_ref[ci * rc:(ci + 1) * rc, :] = y * jax.nn.sigmoid(y)


def _conv_ln(hist, glu, w, b, g, beta, *, bs, tm, rc):
    r, c = glu.shape
    taps = w.shape[0]
    hb = hist.shape[0]
    n_tiles = r // tm
    vec = lambda a: pl.BlockSpec(a.shape, lambda i: (0, 0))
    kern = functools.partial(_conv_ln_kernel, taps=taps, bs=bs, tm=tm, hb=hb, rc=rc, multi=n_tiles > 1)
    args = [hist] + ([glu] if n_tiles > 1 else []) + [glu, w, b, g, beta]
    return pl.pallas_call(
        kern,
        grid=(n_tiles,),
        in_specs=_window_specs(hist, c, tm, hb, n_tiles) + [vec(w), vec(b), vec(g), vec(beta)],
        out_specs=pl.BlockSpec((tm, c), lambda i: (i, 0)),
        out_shape=jax.ShapeDtypeStruct((r, c), F32),
        scratch_shapes=[pltpu.VMEM((hb + tm, c), F32)]
                       + ([pltpu.VMEM((SUBLANES, hb + tm, c), F32)] if _tap_residues(taps, bs, hb) else []),
        compiler_params=_cparams(("arbitrary",)),
        name="conv_ln",
    )(*args)


def _sortable(x):
    bits = pltpu.bitcast(x, I32)
    return bits ^ ((bits >> 31) & 0x7FFFFFFF)


def _select_topk(keys_ref, vals_ref, cand_ref, nsub, limit, topk, demote_from):
    zero8 = jnp.zeros((SUBLANES, LANES), I32)

    def count_in(ref, n, pred):
        def body(s, acc):
            r0 = pl.multiple_of(s * SUB, SUB)
            kk = ref[pl.ds(r0, SUB), :]
            hit = jnp.where(pred(kk, r0), 1, 0)
            return acc + hit.reshape(SUB // SUBLANES, SUBLANES, LANES).sum(axis=0)
        acc = lax.fori_loop(0, n, body, zero8)
        return jnp.sum(acc, axis=0, keepdims=True)

    def count(pred):
        return count_in(keys_ref, nsub, pred)

    def kth_largest(ref, n, lo=None, hi=None):
        ge = lambda cand_u: count_in(ref, n, lambda kk, r0: kk >= (cand_u ^ INT_MIN))
        if lo is None:
            tau_u, nbits = jnp.zeros((1, LANES), I32), 32
        else:
            width = 32 - lax.clz(lo ^ hi)
            tau_u = jnp.where(width >= 32, 0, (hi ^ INT_MIN) & jnp.left_shift(-1, jnp.minimum(width, 31)))
            nbits = jnp.max(width)

        def bit_body(b, tau_u):
            cand = tau_u | jnp.left_shift(jnp.int32(1), nbits - 1 - b)
            return jnp.where(ge(cand) >= topk, cand, tau_u)

        return lax.fori_loop(0, nbits, bit_body, tau_u) ^ INT_MIN

    def prefiltered():
        src_ref = keys_ref if vals_ref is None else vals_ref
        lowest = INT_MIN if vals_ref is None else -jnp.inf
        to_key = (lambda v: v) if vals_ref is None else _sortable
        worst = jnp.full((SUBLANES, LANES), INT_MIN, I32)
        top = worst
        lvl = -(-topk // (N_PHASE * SUBLANES)) - 1
        assert lvl < KEEP
        low = jnp.full((SUBLANES, LANES), 2 ** 31 - 1, I32)
        for ph2 in range(N_PHASE // 2):
            def body(s, ls):
                ls = list(ls)
                r0 = pl.multiple_of(s * SUB, SUB)
                for slab in range(SUB // SLAB):
                    for k in range(2):
                        x = src_ref[pl.ds(r0 + slab * SLAB + (2 * ph2 + k) * SUBLANES, SUBLANES), :]
                        for i in range(KEEP):
                            cur = ls[k * KEEP + i]
                            if vals_ref is None:
                                up = x > cur
                                ls[k * KEEP + i] = jnp.where(up, x, cur)
                                x = jnp.where(up, cur, x)
                            else:
                                ls[k * KEEP + i] = jnp.maximum(cur, x)
                                x = jnp.minimum(cur, x)
                return tuple(ls)

            init = tuple(jnp.full((SUBLANES, LANES), lowest, src_ref.dtype) for _ in range(2 * KEEP))
            ls = [to_key(v) for v in lax.fori_loop(0, nsub, body, init)]
            for k in range(2):
                for i in range(KEEP):
                    c0 = ((2 * ph2 + k) * KEEP + i) * SUBLANES
                    cand_ref[c0:c0 + SUBLANES, :] = ls[k * KEEP + i]
                worst = jnp.maximum(worst, ls[k * KEEP + KEEP - 1])
                top = jnp.maximum(top, ls[k * KEEP])
                low = jnp.minimum(low, ls[k * KEEP + lvl])
        tau = kth_largest(cand_ref, CAND_ROWS // SUB, jnp.min(low, axis=0, keepdims=True),
                          jnp.max(top, axis=0, keepdims=True))
        unsafe = jnp.max(worst, axis=0, keepdims=True) >= tau
        return tau, jnp.max(jnp.where(unsafe, 1, 0))

    tau, unsafe = lax.cond(nsub * (SUB // SLAB) > KEEP, prefiltered,
                           lambda: (jnp.zeros((1, LANES), I32), jnp.int32(1)))
    tau = lax.cond(unsafe > 0, lambda: kth_largest(keys_ref, nsub), lambda: tau)
    n_ge = count(lambda kk, r0: kk >= tau)

    row_iota = lax.broadcasted_iota(I32, (SUB, LANES), 0)

    @pl.when(jnp.max(n_ge) > topk)
    def _ties():
        n_gt = count(lambda kk, r0: kk > tau)
        keep = topk - n_gt

        def pos_body(b, q):
            cand = q | jnp.left_shift(jnp.int32(1), 30 - b)
            n = count(lambda kk, r0: jnp.logical_and(kk == tau, (r0 + row_iota) < cand))
            return jnp.where(n < keep, cand, q)

        last = lax.fori_loop(0, 31, pos_body, jnp.zeros((1, LANES), I32))
        lower = jnp.where(tau == INT_MIN, INT_MIN, tau - 1)

        def demote(s, c):
            r0 = pl.multiple_of(s * SUB, SUB)
            kk = keys_ref[pl.ds(r0, SUB), :]
            drop = jnp.logical_and(kk == tau, (r0 + row_iota) > last)
            keys_ref[pl.ds(r0, SUB), :] = jnp.where(drop, lower, kk)
            return c

        lax.fori_loop(0, nsub, demote, 0)

    def causal(s, c):
        r0 = pl.multiple_of(s * SUB, SUB)
        kk = keys_ref[pl.ds(r0, SUB), :]
        keys_ref[pl.ds(r0, SUB), :] = jnp.where((r0 + row_iota) <= limit, kk, INT_MIN)
        return c

    lax.fori_loop(demote_from, nsub, causal, 0)
    return jnp.maximum(tau, INT_MIN + 1)


def _pair_block_diag(x):
    xf = x.astype(F32)
    lane = lax.broadcasted_iota(I32, xf.shape, 1)
    return jnp.concatenate([jnp.where(lane < HEAD_DIM, xf, 0.0), jnp.where(lane >= HEAD_DIM, xf, 0.0)],
                           axis=0).astype(BF16)


def _attn_prompt_kernel(qb_tab, kt_tab, qi_ref, wt_ref, ki2_ref, q_ref, k_ref, vt_ref, o_ref,
                        keys_ref, thr_ref, m_ref, l_ref, acc_ref, bias_ref, cand_ref, dots_ref, vals_ref,
                        qidx_ref, qatt_ref, *, topk):
    i = pl.program_id(0)
    qb = qb_tab[i]
    kt = kt_tab[i]
    q_end = (qb + 1) * QB

    @pl.when(kt == 0)
    def _select():
        nsub = (q_end + SUB - 1) // SUB
        qpos = qb * QB + lax.broadcasted_iota(I32, (1, LANES), 1)
        row_iota = lax.broadcasted_iota(I32, (SUB, LANES), 0)
        for p in range(N_HEADS // 2):
            qidx_ref[p] = _pair_block_diag(qi_ref[:, p * LANES:(p + 1) * LANES])
            qatt_ref[p] = _pair_block_diag(q_ref[:, p * LANES:(p + 1) * LANES])

        def head_dots(s, slot):
            s = jnp.minimum(s, nsub - 1)
            for half in range(2):
                off = half * (SUB // 2)
                kis = ki2_ref[pl.ds(pl.multiple_of(s * SUB + off, SUB // 2), SUB // 2), :]
                for p in range(IDX_HEADS // 2):
                    cols = slice(p * 2 * QB, (p + 1) * 2 * QB)
                    dots_ref[slot * SUB + off:slot * SUB + off + SUB // 2, cols] = _dot_nt(kis, qidx_ref[p])

        def head_sum(s, slot):
            r0 = pl.multiple_of(jnp.minimum(s, nsub - 1) * SUB, SUB)
            acc = jnp.zeros((SUB, LANES), F32)
            for h in range(IDX_HEADS):
                acc = acc + wt_ref[h:h + 1, :] * jnp.maximum(dots_ref[slot * SUB:(slot + 1) * SUB, h * QB:(h + 1) * QB], 0.0)
            acc = jnp.where(acc == 0.0, 0.0, acc)
            sc = jnp.where((r0 + row_iota) <= qpos, acc, NEG)
            vals_ref[pl.ds(r0, SUB), :] = sc
            keys_ref[pl.ds(r0, SUB), :] = _sortable(sc)

        def scores(j, c):
            head_sum(2 * j, 0)
            head_dots(2 * j + 2, 0)
            head_sum(2 * j + 1, 1)
            head_dots(2 * j + 3, 1)
            return c

        head_dots(0, 0)
        head_dots(1, 1)
        lax.fori_loop(0, (nsub + 1) // 2, scores, 0)
        thr = _select_topk(keys_ref, vals_ref, cand_ref, nsub, qpos, topk, (qb * QB) // SUB)
        thr_ref[...] = jnp.broadcast_to(thr, thr_ref.shape)

        def clear(s, c):
            r0 = pl.multiple_of(s * SUB, SUB)
            keys_ref[pl.ds(r0, SUB), :] = jnp.full((SUB, LANES), INT_MIN, I32)
            return c

        lax.fori_loop(nsub, ((q_end + HT - 1) // HT) * (HT // SUB), clear, 0)
        m_ref[...] = jnp.full(m_ref.shape, NEG, F32)
        l_ref[...] = jnp.zeros(l_ref.shape, F32)
        acc_ref[...] = jnp.zeros(acc_ref.shape, F32)

    def attend(j):
        r0 = pl.multiple_of(kt * KT + j * HT, HT)
        bias_ref[...] = jnp.where(keys_ref[pl.ds(r0, HT), :] >= thr_ref[0:1, :], 0.0, 2 * NEG)
        parts = [slice(i * (HT // 2), (i + 1) * (HT // 2)) for i in range(2)]
        rows_of = lambda part: slice(j * HT + part.start, j * HT + part.stop)

        def qk(p):
            return [_dot_nt(k_ref[rows_of(pt), p * LANES:(p + 1) * LANES], qatt_ref[p]) for pt in parts]

        def softmax(h, sts):
            hh = h % 2
            scs = [st[:, hh * QB:(hh + 1) * QB] + bias_ref[pt, :] for st, pt in zip(sts, parts)]
            m_old = m_ref[h:h + 1, :]
            m_new = jnp.maximum(m_old, jnp.maximum(jnp.max(scs[0], axis=0, keepdims=True),
                                                   jnp.max(scs[1], axis=0, keepdims=True)))
            alpha = jnp.exp2(m_old - m_new)
            pms = [jnp.exp2(sc - m_new) for sc in scs]
            l_ref[h:h + 1, :] = (alpha * l_ref[h:h + 1, :] + jnp.sum(pms[0], axis=0, keepdims=True)
                                 + jnp.sum(pms[1], axis=0, keepdims=True))
            m_ref[h:h + 1, :] = m_new
            return alpha, [pm.astype(BF16) for pm in pms]

        def pv(h, alpha, pms):
            rows = slice(h * HEAD_DIM, (h + 1) * HEAD_DIM)
            out = _dot(vt_ref[rows, rows_of(parts[0])], pms[0]) + _dot(vt_ref[rows, rows_of(parts[1])], pms[1])
            acc_ref[rows, :] = alpha * acc_ref[rows, :] + out

        nxt = qk(0)
        pending = None
        for h in range(N_HEADS):
            if h % 2 == 0:
                sts = nxt
                if h + 2 < N_HEADS:
                    nxt = qk(h // 2 + 1)
            cur = (h,) + softmax(h, sts)
            if pending is not None:
                pv(*pending)
            pending = cur
        pv(*pending)

    attend(0)

    @pl.when(kt * KT + HT < q_end)
    def _second_half():
        attend(1)

    @pl.when((kt + 1) * KT >= q_end)
    def _finish():
        for h in range(N_HEADS):
            rows = slice(h * HEAD_DIM, (h + 1) * HEAD_DIM)
            acc_ref[rows, :] = acc_ref[rows, :] / l_ref[h:h + 1, :]
        o_ref[...] = acc_ref[...].T


def _attn_prompt(qi_b, wi, ki_b, q_b, k_b, v_b, topk):
    t = qi_b.shape[0]
    nb = t // QB
    ki2 = jnp.concatenate([ki_b, ki_b], axis=1)
    wt = wi.T
    vt = v_b.T

    steps = [(b, j) for b in range(nb) for j in range(((b + 1) * QB + KT - 1) // KT)]
    qb_tab = jnp.asarray(np.array([s[0] for s in steps], np.int32))
    kt_tab = jnp.asarray(np.array([s[1] for s in steps], np.int32))
    t_pad = ((t + KT - 1) // KT) * KT

    grid_spec = pltpu.PrefetchScalarGridSpec(
        num_scalar_prefetch=2,
        grid=(len(steps),),
        in_specs=[
            pl.BlockSpec((QB, IDX_HEADS * IDX_DIM), lambda i, qb, kt: (qb[i], 0)),
            pl.BlockSpec((IDX_HEADS, QB), lambda i, qb, kt: (0, qb[i])),
            pl.BlockSpec((t, 2 * IDX_DIM), lambda i, qb, kt: (0, 0)),
            pl.BlockSpec((QB, D_ATTN), lambda i, qb, kt: (qb[i], 0)),
            pl.BlockSpec((KT, D_ATTN), lambda i, qb, kt: (kt[i], 0)),
            pl.BlockSpec((D_ATTN, KT), lambda i, qb, kt: (0, kt[i])),
        ],
        out_specs=pl.BlockSpec((QB, D_ATTN), lambda i, qb, kt: (qb[i], 0)),
        scratch_shapes=[
            pltpu.VMEM((t_pad, LANES), I32),
            pltpu.VMEM((SUBLANES, LANES), I32),
            pltpu.VMEM((N_HEADS, LANES), F32),
            pltpu.VMEM((N_HEADS, LANES), F32),
            pltpu.VMEM((D_ATTN, LANES), F32),
            pltpu.VMEM((HT, LANES), F32),
            pltpu.VMEM((CAND_ROWS, LANES), I32),
            pltpu.VMEM((2 * SUB, IDX_HEADS * QB), F32),
            pltpu.VMEM((t_pad, LANES), F32),
            pltpu.VMEM((IDX_HEADS // 2, 2 * QB, LANES), BF16),
            pltpu.VMEM((N_HEADS // 2, 2 * QB, LANES), BF16),
        ],
    )
    return pl.pallas_call(
        functools.partial(_attn_prompt_kernel, topk=topk),
        grid_spec=grid_spec,
        out_shape=jax.ShapeDtypeStruct((t, D_ATTN), F32),
        compiler_params=_cparams(("arbitrary",)),
        name="attn_prompt",
    )(qb_tab, kt_tab, qi_b, wt, ki2, q_b, k_b, vt)


def _mix_up_kernel(x_ref, cy_ref, ay_ref, gate_ref, shift_ref, scale_ref, gpm_ref, gpf_ref, wo_ref, wu_ref,
                   x1_ref, up_ref):
    mix = _dot(cy_ref[...].astype(BF16), wo_ref[0:D_CONV, :]) + _dot(ay_ref[...].astype(BF16), wo_ref[D_CONV:, :])
    x1 = x_ref[...] + gate_ref[...] * _rms(mix, gpm_ref[...])
    x1_ref[...] = x1
    h2 = (_rms(x1, gpf_ref[...]) * (1.0 + scale_ref[...]) + shift_ref[...]).astype(BF16)
    for c in range(up_ref.shape[1] // 512):
        up_ref[:, c * 512:(c + 1) * 512] = _dot(h2, wu_ref[:, c * 512:(c + 1) * 512])


def _mix_up(x, cy, ay, gate1, shift2, scale2, g_post_mix, g_pre_ffn, w_out, w_up, tm):
    r, d = x.shape
    nf = w_up.shape[1]
    per_row = gate1.shape[0] != 1
    mod_spec = pl.BlockSpec((tm, d), lambda i: (i, 0)) if per_row else pl.BlockSpec((1, d), lambda i: (0, 0))
    row = lambda n: pl.BlockSpec((tm, n), lambda i: (i, 0))
    full = lambda a: pl.BlockSpec(a.shape, lambda i: (0, 0))
    return pl.pallas_call(
        _mix_up_kernel,
        grid=(r // tm,),
        in_specs=[row(d), row(D_CONV), row(D_ATTN), mod_spec, mod_spec, mod_spec, full(g_post_mix), full(g_pre_ffn),
                  full(w_out), full(w_up)],
        out_specs=[row(d), row(nf)],
        out_shape=[jax.ShapeDtypeStruct((r, d), F32), jax.ShapeDtypeStruct((r, nf), F32)],
        compiler_params=_cparams(("arbitrary",)),
        name="mix_up",
    )(x, cy, ay, gate1, shift2, scale2, g_post_mix, g_pre_ffn, w_out, w_up)


def _ffn_down_kernel(*refs, bs, tm, hb, multi):
    if multi:
        hist_ref, halo_ref, main_ref, x1_ref, gate_ref, w_ref, b_ref, wd_ref, g_ref, o_ref, win_ref = refs
    else:
        hist_ref, main_ref, x1_ref, gate_ref, w_ref, b_ref, wd_ref, g_ref, o_ref, win_ref = refs
        halo_ref = None
    _fill_window(win_ref, hist_ref, halo_ref, main_ref, hb, tm)
    taps = w_ref.shape[0]
    base = hb - (taps - 1) * bs
    cw = 256

    def conv(c0):
        acc = jnp.broadcast_to(b_ref[:, c0:c0 + cw], (tm, cw))
        for j in range(taps):
            acc = acc + w_ref[j:j + 1, c0:c0 + cw] * win_ref[base + j * bs:base + j * bs + tm, c0:c0 + cw]
        return acc

    f = jnp.zeros((tm, o_ref.shape[1]), F32)
    for c in range(D_FF // cw):
        a = conv(c * cw)
        g = conv(D_FF + c * cw)
        gated = (g * jax.nn.sigmoid(g) * a).astype(BF16)
        f = f + _dot(gated, wd_ref[c * cw:(c + 1) * cw, :])
    o_ref[...] = x1_ref[...] + gate_ref[...] * _rms(f, g_ref[...])


def _ffn_down(hist, up, x1, gate2, w, b, w_down, g, *, bs, tm):
    r, nf = up.shape
    d = x1.shape[1]
    hb = hist.shape[0]
    n_tiles = r // tm
    per_row = gate2.shape[0] != 1
    mod_spec = pl.BlockSpec((tm, d), lambda i: (i, 0)) if per_row else pl.BlockSpec((1, d), lambda i: (0, 0))
    full = lambda a: pl.BlockSpec(a.shape, lambda i: (0, 0))
    kern = functools.partial(_ffn_down_kernel, bs=bs, tm=tm, hb=hb, multi=n_tiles > 1)
    args = [hist] + ([up] if n_tiles > 1 else []) + [up, x1, gate2, w, b, w_down, g]
    return pl.pallas_call(
        kern,
        grid=(n_tiles,),
        in_specs=_window_specs(hist, nf, tm, hb, n_tiles) + [pl.BlockSpec((tm, d), lambda i: (i, 0)), mod_spec,
                                                              full(w), full(b), full(w_down), full(g)],
        out_specs=pl.BlockSpec((tm, d), lambda i: (i, 0)),
        out_shape=jax.ShapeDtypeStruct((r, d), F32),
        scratch_shapes=[pltpu.VMEM((hb + tm, nf), F32)],
        compiler_params=_cparams(("arbitrary",)),
        name="ffn_down",
    )(*args)


def _sample_scores_kernel(pt_ref, q_ref, w_ref, knew_ref, *refs, n_chunks, past, n_new):
    page_refs, o_ref = refs[:PPC], refs[PPC]
    c = pl.program_id(1)
    q = q_ref[...]
    w = w_ref[...]

    def head_sum(page):
        sc = _dot(q, page.astype(BF16))
        val = w * jnp.maximum(sc, 0.0)
        tot = val.reshape(IDX_HEADS, SUBLANES, LANES).sum(axis=0)
        return jnp.where(tot == 0.0, 0.0, tot)

    tq = lax.broadcasted_iota(I32, (SUBLANES, LANES), 0)
    j = lax.broadcasted_iota(I32, (SUBLANES, LANES), 1)

    @pl.when(c < n_chunks)
    def _past():
        for r in range(PPC):
            pos = (c * PPC + r) * PAGE_SIZE + j
            o_ref[:, r * LANES:(r + 1) * LANES] = jnp.where(tq < n_new, head_sum(page_refs[r][...]), -pos.astype(F32))

    @pl.when(c == n_chunks)
    def _new():
        valid = jnp.logical_and(j <= tq, j < n_new)
        o_ref[:, 0:LANES] = jnp.where(valid, head_sum(knew_ref[...]), NEG)
        o_ref[:, LANES:] = jnp.full((SUBLANES, (PPC - 1) * LANES), NEG, F32)


def _sample_scores(page_table, q_rows, w_rows, kit_new, cache_kit, layer):
    nb, n_pages = page_table.shape
    n_chunks = n_pages // PPC
    past = n_pages * PAGE_SIZE
    cw = PPC * PAGE_SIZE

    def page_spec(r):
        return pl.BlockSpec((None, None, IDX_DIM, PAGE_SIZE),
                            lambda b, c, pt: (layer, pt[b, jnp.minimum(c * PPC + r, n_pages - 1)], 0, 0))

    grid_spec = pltpu.PrefetchScalarGridSpec(
        num_scalar_prefetch=1,
        grid=(nb, n_chunks + 1),
        in_specs=[pl.BlockSpec((None, IDX_HEADS * SUBLANES, IDX_DIM), lambda b, c, pt: (b, 0, 0)),
                  pl.BlockSpec((None, IDX_HEADS * SUBLANES, LANES), lambda b, c, pt: (b, 0, 0)),
                  pl.BlockSpec((None, IDX_DIM, PAGE_SIZE), lambda b, c, pt: (b, 0, 0))]
                 + [page_spec(r) for r in range(PPC)],
        out_specs=pl.BlockSpec((None, SUBLANES, cw), lambda b, c, pt: (b, 0, c)),
    )
    return pl.pallas_call(
        functools.partial(_sample_scores_kernel, n_chunks=n_chunks, past=past, n_new=4),
        grid_spec=grid_spec,
        out_shape=jax.ShapeDtypeStruct((nb, SUBLANES, past + cw), F32),
        compiler_params=_cparams(("arbitrary", "arbitrary")),
        name="sample_scores",
    )(page_table, q_rows, w_rows, kit_new, *([cache_kit] * PPC))


def _sample_select_kernel(sc_ref, lim_ref, keys_ref, thr_ref, cand_ref, *, topk):
    rows = sc_ref.shape[0]

    def conv(s, c):
        r0 = pl.multiple_of(s * SUB, SUB)
        keys_ref[pl.ds(r0, SUB), :] = _sortable(sc_ref[pl.ds(r0, SUB), :])
        return c

    lax.fori_loop(0, rows // SUB, conv, 0)
    thr = _select_topk(keys_ref, None, cand_ref, rows // SUB, lim_ref[...], topk, 0)
    thr_ref[...] = jnp.broadcast_to(thr, thr_ref.shape)


def _sample_select(scores_t, limit, topk):
    rows, nq = scores_t.shape
    return pl.pallas_call(
        functools.partial(_sample_select_kernel, topk=topk),
        grid=(nq // LANES,),
        in_specs=[pl.BlockSpec((rows, LANES), lambda g: (0, g)), pl.BlockSpec((1, LANES), lambda g: (0, g))],
        out_specs=[pl.BlockSpec((rows, LANES), lambda g: (0, g)), pl.BlockSpec((SUBLANES, LANES), lambda g: (0, g))],
        out_shape=[jax.ShapeDtypeStruct((rows, nq), I32), jax.ShapeDtypeStruct((SUBLANES, nq), I32)],
        scratch_shapes=[pltpu.VMEM((CAND_ROWS, LANES), I32)],
        compiler_params=_cparams(("arbitrary",)),
        name="sample_select",
    )(scores_t, limit)


def _sample_attn_kernel(pt_ref, q_ref, keys_ref, thr_ref, knew_ref, vnew_ref, *refs, n_chunks):
    k_refs, v_refs = refs[:APC], refs[APC:2 * APC]
    o_ref, m_ref, l_ref, acc_ref = refs[2 * APC:]
    c = pl.program_id(1)

    @pl.when(c == 0)
    def _init():
        m_ref[...] = jnp.full(m_ref.shape, NEG, F32)
        l_ref[...] = jnp.zeros(l_ref.shape, F32)
        acc_ref[...] = jnp.zeros(acc_ref.shape, F32)

    thr = thr_ref[...]

    def pages_update(pages):
        n = len(pages)
        bias = jnp.where(keys_ref[:, 0:n * LANES] >= thr[:, 0:1], 0.0, 2 * NEG)
        scs = []
        for h in range(N_HEADS):
            kt = jnp.concatenate([kp[h].astype(BF16) for kp, _ in pages], axis=1)
            scs.append(_dot(q_ref[h * SUBLANES:(h + 1) * SUBLANES, :], kt) + bias)
        sc = jnp.concatenate(scs, axis=0)
        m_old = m_ref[...]
        m_new = jnp.maximum(m_old, jnp.max(sc, axis=1, keepdims=True))
        alpha = jnp.exp2(m_old - m_new)
        pm = jnp.exp2(sc - m_new)
        l_ref[...] = alpha * l_ref[...] + jnp.sum(pm, axis=1, keepdims=True)
        pb = pm.astype(BF16)
        outs = []
        for h in range(N_HEADS):
            vt = jnp.concatenate([vp[h].astype(BF16) for _, vp in pages], axis=1)
            outs.append(_dot_nt(pb[h * SUBLANES:(h + 1) * SUBLANES, :], vt))
        acc_ref[...] = alpha * acc_ref[...] + jnp.concatenate(outs, axis=0)
        m_ref[...] = m_new

    @pl.when(c < n_chunks)
    def _past():
        pages_update(list(zip(k_refs, v_refs)))

    @pl.when(c == n_chunks)
    def _new():
        pages_update([(knew_ref, vnew_ref)])
        out = acc_ref[...] / l_ref[...]
        o_ref[...] = jnp.concatenate([out[h * SUBLANES:(h + 1) * SUBLANES, :] for h in range(N_HEADS)], axis=1)


def _sample_attn(page_table, q_rows, keys, thr, kt_new, vt_new, cache_kt, cache_vt, layer):
    nb, n_pages = page_table.shape
    n_chunks = n_pages // APC
    cw = APC * PAGE_SIZE
    rows = N_HEADS * SUBLANES

    def page_spec(r):
        return pl.BlockSpec((None, None, N_HEADS, HEAD_DIM, PAGE_SIZE),
                            lambda b, c, pt: (layer, pt[b, jnp.minimum(c * APC + r, n_pages - 1)], 0, 0, 0))

    new_spec = pl.BlockSpec((None, N_HEADS, HEAD_DIM, PAGE_SIZE), lambda b, c, pt: (b, 0, 0, 0))
    grid_spec = pltpu.PrefetchScalarGridSpec(
        num_scalar_prefetch=1,
        grid=(nb, n_chunks + 1),
        in_specs=[pl.BlockSpec((None, rows, HEAD_DIM), lambda b, c, pt: (b, 0, 0)),
                  pl.BlockSpec((None, SUBLANES, cw), lambda b, c, pt: (b, 0, c)),
                  pl.BlockSpec((None, SUBLANES, LANES), lambda b, c, pt: (b, 0, 0)),
                  new_spec, new_spec]
                 + [page_spec(r) for r in range(APC)] * 2,
        out_specs=pl.BlockSpec((None, SUBLANES, D_ATTN), lambda b, c, pt: (b, 0, 0)),
        scratch_shapes=[pltpu.VMEM((rows, 1), F32), pltpu.VMEM((rows, 1), F32), pltpu.VMEM((rows, HEAD_DIM), F32)],
    )
    return pl.pallas_call(
        functools.partial(_sample_attn_kernel, n_chunks=n_chunks),
        grid_spec=grid_spec,
        out_shape=jax.ShapeDtypeStruct((nb, SUBLANES, D_ATTN), F32),
        compiler_params=_cparams(("arbitrary", "arbitrary")),
        name="sample_attn",
    )(page_table, q_rows, keys, thr, kt_new, vt_new, *([cache_kt] * APC), *([cache_vt] * APC))


def _split_mod(mod):
    return [mod[:, i * D_MODEL:(i + 1) * D_MODEL] for i in range(6)]


def _layer_prompt(x, mod, wts):
    t = x.shape[0]
    shift1, scale1, gate1, shift2, scale2, gate2 = _split_mod(mod)
    glu, q_b, k, k_b, v, v_b, qi_b, tail = _inproj(x, shift1, scale1, wts["g_pre_mix"], wts["w_main"], wts["w_tail"], 512)
    ki = tail[:, :IDX_DIM]
    wi = tail[:, IDX_DIM:IDX_DIM + IDX_HEADS]
    conv_hist = jnp.zeros((32, D_CONV), F32)
    conv_y = _conv_ln(conv_hist, glu, wts["w_dw"], wts["b_dw"], wts["ln_g"], wts["ln_b"], bs=1, tm=256, rc=32)
    attn_y = _attn_prompt(qi_b, wi, ki.astype(BF16), q_b, k_b, v_b, min(TOPK_MAX, t // 4))
    x1, up = _mix_up(x, conv_y, attn_y, gate1, shift2, scale2, wts["g_post_mix"], wts["g_pre_ffn"],
                     wts["w_out"], wts["w_up"], 512)
    ffn_hist = jnp.zeros((SUBLANES, 2 * D_FF), F32)
    y = _ffn_down(ffn_hist, up, x1, gate2, wts["w_ffn_dw"], wts["b_ffn_dw"], wts["w_down"], wts["g_post_ffn"],
                  bs=1, tm=256)
    return y, k, v, ki, glu[t - (CONV_WIDTH - 1):], up[t - (FFN_CONV_WIDTH - 1):]


def _layer_sample(x, mod, conv_state, ffn_state, cache_kt, cache_vt, cache_kit, page_table, layer, wts):
    nb, n_pages = page_table.shape
    r = x.shape[0]
    nt = r // nb
    past = n_pages * PAGE_SIZE
    shift1, scale1, gate1, shift2, scale2, gate2 = [jnp.tile(m, (nt, 1)) for m in _split_mod(mod)]
    glu, q_b, k, k_b, v, v_b, qi_b, tail = _inproj(x, shift1, scale1, wts["g_pre_mix"], wts["w_main"], wts["w_tail"], r)
    ki = tail[:, :IDX_DIM]
    wi = tail[:, IDX_DIM:IDX_DIM + IDX_HEADS]
    conv_hist = conv_state.transpose(1, 0, 2).reshape((CONV_WIDTH - 1) * nb, D_CONV)
    conv_y = _conv_ln(conv_hist, glu, wts["w_dw"], wts["b_dw"], wts["ln_g"], wts["ln_b"], bs=nb, tm=r, rc=32)

    tb = lambda a: a.reshape(nt, nb, -1).transpose(1, 0, 2)
    pad_t = lambda a: jnp.pad(a, ((0, 0), (0, SUBLANES - nt)) + ((0, 0),) * (a.ndim - 2))
    qi_s = pad_t(tb(qi_b).reshape(nb, nt, IDX_HEADS, IDX_DIM)).transpose(0, 2, 1, 3)
    q_rows = qi_s.reshape(nb, IDX_HEADS * SUBLANES, IDX_DIM)
    w_s = pad_t(tb(wi)).transpose(0, 2, 1).reshape(nb, IDX_HEADS * SUBLANES, 1)
    w_rows = jnp.broadcast_to(w_s, (nb, IDX_HEADS * SUBLANES, LANES))
    pad_keys = lambda a: jnp.pad(tb(a), ((0, 0), (0, PAGE_SIZE - nt), (0, 0)))
    kit_new = pad_keys(ki).transpose(0, 2, 1)
    scores = _sample_scores(page_table, q_rows, w_rows, kit_new, cache_kit, layer)
    total = scores.shape[2]
    topk = min(TOPK_MAX, (past + nt) // 4)
    scores_t = scores.reshape(nb * SUBLANES, total).T
    limit = jnp.tile(past + jnp.arange(SUBLANES, dtype=I32), nb).reshape(1, nb * SUBLANES)
    keys_t, thr = _sample_select(scores_t, limit, topk)
    keys = keys_t.T.reshape(nb, SUBLANES, total)
    thr_rows = jnp.broadcast_to(thr[0].reshape(nb, SUBLANES, 1), (nb, SUBLANES, LANES))

    q_s = pad_t(tb(q_b).reshape(nb, nt, N_HEADS, HEAD_DIM)).transpose(0, 2, 1, 3)
    new_t = lambda a: pad_keys(a).reshape(nb, PAGE_SIZE, N_HEADS, HEAD_DIM).transpose(0, 2, 3, 1)
    attn = _sample_attn(page_table, q_s.reshape(nb, N_HEADS * SUBLANES, HEAD_DIM), keys, thr_rows,
                        new_t(k), new_t(v), cache_kt, cache_vt, layer)
    attn_y = attn[:, :nt].transpose(1, 0, 2).reshape(r, D_ATTN)

    x1, up = _mix_up(x, conv_y, attn_y, gate1, shift2, scale2, wts["g_post_mix"], wts["g_pre_ffn"],
                     wts["w_out"], wts["w_up"], r)
    ffn_hist = ffn_state.transpose(1, 0, 2).reshape((FFN_CONV_WIDTH - 1) * nb, 2 * D_FF)
    y = _ffn_down(ffn_hist, up, x1, gate2, wts["w_ffn_dw"], wts["b_ffn_dw"], wts["w_down"], wts["g_post_ffn"],
                  bs=nb, tm=r)
    conv_all = jnp.concatenate([conv_hist, glu], axis=0).reshape(CONV_WIDTH - 1 + nt, nb, D_CONV)
    ffn_all = jnp.concatenate([ffn_hist, up], axis=0).reshape(FFN_CONV_WIDTH - 1 + nt, nb, 2 * D_FF)
    return (y, k, v, ki, conv_all[nt:].transpose(1, 0, 2), ffn_all[nt:].transpose(1, 0, 2))


def kernel(x_prompt, x_sample, cache_k, cache_v, cache_kidx, state_conv, state_ffn, page_table, c_prompt, c_sample, w_ada, b_ada, g_pre_mix, w_in, w_dw, b_dw, ln_g, ln_b, w_out, g_post_mix, g_pre_ffn, w_up, w_ffn_dw, b_ffn_dw, w_down, g_post_ffn):
    depth = w_ada.shape[0]
    pb, seq, d = x_prompt.shape
    db, dt, _ = x_sample.shape
    assert pb == 1 and d == D_MODEL and seq % KT == 0 and db * dt == LANES and dt <= SUBLANES
    cache_kt = cache_k.transpose(0, 1, 3, 4, 2)
    cache_vt = cache_v.transpose(0, 1, 3, 4, 2)
    cache_kit = cache_kidx.transpose(0, 1, 3, 2)

    xp = x_prompt[0]
    xs = x_sample.transpose(1, 0, 2).reshape(dt * db, d)
    n_c = pb + db
    c_all = jnp.pad(jnp.concatenate([c_prompt, c_sample], axis=0), ((0, (-n_c) % SUBLANES), (0, 0)))
    row = lambda a: a.reshape(1, -1)
    outs_p, outs_s = [], []
    for l in range(depth):
        mod = _ada(c_all, w_ada[l], b_ada[l])
        wts = dict(
            g_pre_mix=row(g_pre_mix[l]), w_main=w_in[l][:, :N_MAIN].astype(BF16),
            w_tail=jnp.pad(w_in[l][:, N_MAIN:], ((0, 0), (0, LANES - IDX_DIM - IDX_HEADS))).astype(BF16),
            w_dw=w_dw[l], b_dw=row(b_dw[l]), ln_g=row(ln_g[l]), ln_b=row(ln_b[l]),
            w_out=w_out[l].astype(BF16), g_post_mix=row(g_post_mix[l]), g_pre_ffn=row(g_pre_ffn[l]),
            w_up=w_up[l].astype(BF16), w_ffn_dw=w_ffn_dw[l], b_ffn_dw=row(b_ffn_dw[l]),
            w_down=w_down[l].astype(BF16), g_post_ffn=row(g_post_ffn[l]))
        xp, kp, vp, kip, cp, fp = _layer_prompt(xp, mod[:pb], wts)
        xs, k_s, v_s, ki_s, c_s, f_s = _layer_sample(xs, mod[pb:n_c], state_conv[l], state_ffn[l], cache_kt, cache_vt,
                                                      cache_kit, page_table, l, wts)
        unflat = lambda a, n: a.reshape(dt, db, n).transpose(1, 0, 2)
        outs_p.append((kp.reshape(pb, seq, N_HEADS, HEAD_DIM), vp.reshape(pb, seq, N_HEADS, HEAD_DIM),
                       kip.reshape(pb, seq, IDX_DIM), cp[None], fp[None]))
        outs_s.append((unflat(k_s, D_ATTN).reshape(db, dt, N_HEADS, HEAD_DIM),
                       unflat(v_s, D_ATTN).reshape(db, dt, N_HEADS, HEAD_DIM), unflat(ki_s, IDX_DIM), c_s, f_s))
    stack = lambda outs, i: jnp.stack([o[i] for o in outs])
    y_prompt = xp[None]
    y_sample = xs.reshape(dt, db, d).transpose(1, 0, 2)
    return (y_prompt, y_sample) + tuple(stack(outs_p, i) for i in range(5)) + tuple(stack(outs_s, i) for i in range(5))
```

```python
import functools

import jax
import jax.numpy as jnp
import numpy as np
from jax import lax
from jax.experimental import pallas as pl
from jax.experimental.pallas import tpu as pltpu

F32 = jnp.float32
BF16 = jnp.bfloat16
I32 = jnp.int32

D_MODEL = 1024
D_CONV = 512
CONV_WIDTH = 31
N_HEADS = 8
HEAD_DIM = 64
D_ATTN = N_HEADS * HEAD_DIM
IDX_HEADS = 8
IDX_DIM = 64
TOPK_MAX = 256
D_FF = 2816
FFN_CONV_WIDTH = 3
PAGE_SIZE = 128
EPS = 1e-6
NEG = -1e30
INDEX_WEIGHT_SCALE = (IDX_HEADS * IDX_DIM) ** -0.5
LOG2E = 1.4426950408889634
N_MAIN = 6 * 512
INT_MIN = -(2 ** 31)

LANES = 128
SUBLANES = 8
VMEM_LIMIT = 56 * 1024 * 1024

QB = 128
SUB = 256
HT = 1024
KT = HT
PPC = 16
APC = 16
SLAB = 128
N_PHASE = SLAB // SUBLANES
KEEP = 12
CAND_ROWS = N_PHASE * KEEP * SUBLANES


def _cparams(sem):
    return pltpu.CompilerParams(dimension_semantics=sem, vmem_limit_bytes=VMEM_LIMIT)


def _rms(x, g):
    return x * lax.rsqrt(jnp.mean(x * x, axis=-1, keepdims=True) + EPS) * g


def _dot(a, b):
    return jnp.dot(a, b, preferred_element_type=F32)


def _dot_nt(a, b):
    return lax.dot_general(a, b, (((1,), (1,)), ((), ())), preferred_element_type=F32)


def _ada_kernel(c_ref, w_ref, b_ref, o_ref):
    c = c_ref[...]
    s = (c * jax.nn.sigmoid(c)).astype(BF16)
    o_ref[...] = _dot(s, w_ref[...].astype(BF16)) + b_ref[...]


def _ada(c_all, w_ada, b_ada):
    r, d = c_all.shape
    n = w_ada.shape[1]
    tn = 1024
    return pl.pallas_call(
        _ada_kernel,
        grid=(n // tn,),
        in_specs=[pl.BlockSpec((r, d), lambda j: (0, 0)),
                  pl.BlockSpec((d, tn), lambda j: (0, j)),
                  pl.BlockSpec((1, tn), lambda j: (0, j))],
        out_specs=pl.BlockSpec((r, tn), lambda j: (0, j)),
        out_shape=jax.ShapeDtypeStruct((r, n), F32),
        compiler_params=_cparams(("arbitrary",)),
        name="ada",
    )(c_all, w_ada, b_ada.reshape(1, n))


def _inproj_kernel(x_ref, shift_ref, scale_ref, g_ref, wm_ref, wt_ref,
                   glu_ref, q_ref, k_ref, kb_ref, v_ref, vb_ref, qi_ref, tail_ref):
    h = _rms(x_ref[...], g_ref[...]) * (1.0 + scale_ref[...]) + shift_ref[...]
    hb = h.astype(BF16)

    def proj(i):
        return _dot(hb, wm_ref[:, i * 512:(i + 1) * 512])

    ca = proj(0)
    cg = proj(1)
    glu_ref[...] = ca * jax.nn.sigmoid(cg)
    q_ref[...] = (proj(2) * (HEAD_DIM ** -0.5 * LOG2E)).astype(BF16)
    k = proj(3)
    k_ref[...] = k
    kb_ref[...] = k.astype(BF16)
    v = proj(4)
    v_ref[...] = v
    vb_ref[...] = v.astype(BF16)
    qi_ref[...] = proj(5).astype(BF16)
    lane = lax.broadcasted_iota(I32, (1, LANES), 1)
    tail_ref[...] = _dot(hb, wt_ref[...]) * jnp.where(lane >= IDX_DIM, INDEX_WEIGHT_SCALE, 1.0)


def _inproj(x, shift, scale, g, w_main, w_tail, tm):
    r, d = x.shape
    per_row = shift.shape[0] != 1
    mod_spec = pl.BlockSpec((tm, d), lambda i: (i, 0)) if per_row else pl.BlockSpec((1, d), lambda i: (0, 0))
    row = lambda n: pl.BlockSpec((tm, n), lambda i: (i, 0))
    full = lambda a: pl.BlockSpec(a.shape, lambda i: (0, 0))
    sds = lambda n, dt: jax.ShapeDtypeStruct((r, n), dt)
    return pl.pallas_call(
        _inproj_kernel,
        grid=(r // tm,),
        in_specs=[row(d), mod_spec, mod_spec, full(g), full(w_main), full(w_tail)],
        out_specs=[row(512)] * 7 + [row(LANES)],
        out_shape=[sds(512, F32), sds(512, BF16), sds(512, F32), sds(512, BF16), sds(512, F32), sds(512, BF16),
                   sds(512, BF16), sds(LANES, F32)],
        compiler_params=_cparams(("arbitrary",)),
        name="inproj",
    )(x, shift, scale, g, w_main, w_tail)


def _fill_window(win_ref, hist_ref, halo_ref, main_ref, hb, tm):
    win_ref[hb:hb + tm, :] = main_ref[...]
    if halo_ref is None:
        win_ref[0:hb, :] = hist_ref[...]
    else:
        first = pl.program_id(0) == 0

        @pl.when(first)
        def _():
            win_ref[0:hb, :] = hist_ref[...]

        @pl.when(jnp.logical_not(first))
        def _():
            win_ref[0:hb, :] = halo_ref[...]


def _window_specs(hist, main_cols, tm, hb, n_tiles):
    specs = [pl.BlockSpec(hist.shape, lambda i: (0, 0))]
    if n_tiles > 1:
        per = tm // hb
        specs.append(pl.BlockSpec((hb, main_cols), lambda i: (jnp.maximum(i * per - 1, 0), 0)))
    specs.append(pl.BlockSpec((tm, main_cols), lambda i: (i, 0)))
    return specs


def _tap_residues(taps, bs, hb):
    base = hb - (taps - 1) * bs
    return sorted({(base + j * bs) % SUBLANES for j in range(taps)} - {0})


def _conv_ln_kernel(*refs, taps, bs, tm, hb, rc, multi):
    residues = _tap_residues(taps, bs, hb)
    refs = list(refs)
    sh_ref = refs.pop() if residues else None
    if multi:
        hist_ref, halo_ref, main_ref, w_ref, b_ref, g_ref, beta_ref, o_ref, win_ref = refs
    else:
        hist_ref, main_ref, w_ref, b_ref, g_ref, beta_ref, o_ref, win_ref = refs
        halo_ref = None
    _fill_window(win_ref, hist_ref, halo_ref, main_ref, hb, tm)
    base = hb - (taps - 1) * bs
    c = o_ref.shape[1]
    span = hb + tm - SUBLANES
    for rho in residues:
        sh_ref[rho, 0:span, :] = win_ref[rho:rho + span, :]

    def rows_at(r0):
        rho = r0 % SUBLANES
        return win_ref[r0:r0 + rc, :] if rho == 0 else sh_ref[rho, r0 - rho:r0 - rho + rc, :]

    for ci in range(tm // rc):
        acc = jnp.broadcast_to(b_ref[...], (rc, c))
        for j in range(taps):
            acc = acc + w_ref[j:j + 1, :] * rows_at(base + ci * rc + j * bs)
        mu = jnp.mean(acc, axis=-1, keepdims=True)
        xc = acc - mu
        var = jnp.mean(xc * xc, axis=-1, keepdims=True)
        y = xc * lax.rsqrt(var + EPS) * g_ref[...] + beta_ref[...]
        o_ref[ci * rc:(ci + 1) * rc, :] = y * jax.nn.sigmoid(y)


def _conv_ln(hist, glu, w, b, g, beta, *, bs, tm, rc):
    r, c = glu.shape
    taps = w.shape[0]
    hb = hist.shape[0]
    n_tiles = r // tm
    vec = lambda a: pl.BlockSpec(a.shape, lambda i: (0, 0))
    kern = functools.partial(_conv_ln_kernel, taps=taps, bs=bs, tm=tm, hb=hb, rc=rc, multi=n_tiles > 1)
    args = [hist] + ([glu] if n_tiles > 1 else []) + [glu, w, b, g, beta]
    return pl.pallas_call(
        kern,
        grid=(n_tiles,),
        in_specs=_window_specs(hist, c, tm, hb, n_tiles) + [vec(w), vec(b), vec(g), vec(beta)],
        out_specs=pl.BlockSpec((tm, c), lambda i: (i, 0)),
        out_shape=jax.ShapeDtypeStruct((r, c), F32),
        scratch_shapes=[pltpu.VMEM((hb + tm, c), F32)]
                       + ([pltpu.VMEM((SUBLANES, hb + tm, c), F32)] if _tap_residues(taps, bs, hb) else []),
        compiler_params=_cparams(("arbitrary",)),
        name="conv_ln",
    )(*args)


def _sortable(x):
    bits = pltpu.bitcast(x, I32)
    return bits ^ ((bits >> 31) & 0x7FFFFFFF)


def _select_topk(keys_ref, vals_ref, cand_ref, nsub, limit, topk, demote_from):
    zero8 = jnp.zeros((SUBLANES, LANES), I32)

    def count_in(ref, n, pred):
        def body(s, acc):
            r0 = s * SUB if isinstance(s, int) else pl.multiple_of(s * SUB, SUB)
            kk = ref[pl.ds(r0, SUB), :]
            hit = jnp.where(pred(kk, r0), 1, 0)
            return acc + hit.reshape(SUB // SUBLANES, SUBLANES, LANES).sum(axis=0)
        if isinstance(n, int):
            acc = functools.reduce(lambda a, s: body(s, a), range(n), zero8)
        else:
            acc = lax.fori_loop(0, n, body, zero8)
        return jnp.sum(acc, axis=0, keepdims=True)

    def count(pred):
        return count_in(keys_ref, nsub, pred)

    def kth_largest(ref, n):
        ge = lambda cand_u: count_in(ref, n, lambda kk, r0: kk >= (cand_u ^ INT_MIN))

        def bit_body(b, tau_u):
            cand = tau_u | jnp.left_shift(jnp.int32(1), 31 - b)
            return jnp.where(ge(cand) >= topk, cand, tau_u)

        return lax.fori_loop(0, 32, bit_body, jnp.zeros((1, LANES), I32)) ^ INT_MIN

    def prefiltered():
        src_ref = keys_ref if vals_ref is None else vals_ref
        lowest = INT_MIN if vals_ref is None else -jnp.inf
        to_key = (lambda v: v) if vals_ref is None else _sortable
        worst = jnp.full((SUBLANES, LANES), INT_MIN, I32)
        for ph2 in range(N_PHASE // 2):
            def body(s, ls):
                ls = list(ls)
                r0 = pl.multiple_of(s * SUB, SUB)
                for slab in range(SUB // SLAB):
                    for k in range(2):
                        x = src_ref[pl.ds(r0 + slab * SLAB + (2 * ph2 + k) * SUBLANES, SUBLANES), :]
                        for i in range(KEEP):
                            cur = ls[k * KEEP + i]
                            if vals_ref is None:
                                up = x > cur
                                ls[k * KEEP + i] = jnp.where(up, x, cur)
                                x = jnp.where(up, cur, x)
                            else:
                                ls[k * KEEP + i] = jnp.maximum(cur, x)
                                x = jnp.minimum(cur, x)
                return tuple(ls)

            init = tuple(jnp.full((SUBLANES, LANES), lowest, src_ref.dtype) for _ in range(2 * KEEP))
            ls = [to_key(v) for v in lax.fori_loop(0, nsub, body, init)]
            for k in range(2):
                for i in range(KEEP):
                    c0 = ((2 * ph2 + k) * KEEP + i) * SUBLANES
                    cand_ref[c0:c0 + SUBLANES, :] = ls[k * KEEP + i]
                worst = jnp.maximum(worst, ls[k * KEEP + KEEP - 1])
        n_cand = CAND_ROWS // SUB
        tau = kth_largest(cand_ref, n_cand)
        unsafe = jnp.max(worst, axis=0, keepdims=True) >= tau
        return tau, jnp.max(jnp.where(unsafe, 1, 0)), count_in(cand_ref, n_cand, lambda kk, r0: kk >= tau)

    def exact():
        tau = kth_largest(keys_ref, nsub)
        return tau, count(lambda kk, r0: kk >= tau)

    zero = jnp.zeros((1, LANES), I32)
    tau, unsafe, n_ge = lax.cond(nsub * (SUB // SLAB) > KEEP, prefiltered, lambda: (zero, jnp.int32(1), zero))
    tau, n_ge = lax.cond(unsafe > 0, exact, lambda: (tau, n_ge))

    row_iota = lax.broadcasted_iota(I32, (SUB, LANES), 0)

    @pl.when(jnp.max(n_ge) > topk)
    def _ties():
        n_gt = count(lambda kk, r0: kk > tau)
        keep = topk - n_gt

        def pos_body(b, q):
            cand = q | jnp.left_shift(jnp.int32(1), 30 - b)
            n = count(lambda kk, r0: jnp.logical_and(kk == tau, (r0 + row_iota) < cand))
            return jnp.where(n < keep, cand, q)

        last = lax.fori_loop(0, 31, pos_body, jnp.zeros((1, LANES), I32))
        lower = jnp.where(tau == INT_MIN, INT_MIN, tau - 1)

        def demote(s, c):
            r0 = pl.multiple_of(s * SUB, SUB)
            kk = keys_ref[pl.ds(r0, SUB), :]
            drop = jnp.logical_and(kk == tau, (r0 + row_iota) > last)
            keys_ref[pl.ds(r0, SUB), :] = jnp.where(drop, lower, kk)
            return c

        lax.fori_loop(0, nsub, demote, 0)

    def causal(s, c):
        r0 = pl.multiple_of(s * SUB, SUB)
        kk = keys_ref[pl.ds(r0, SUB), :]
        keys_ref[pl.ds(r0, SUB), :] = jnp.where((r0 + row_iota) <= limit, kk, INT_MIN)
        return c

    lax.fori_loop(demote_from, nsub, causal, 0)
    return jnp.maximum(tau, INT_MIN + 1)


def _pair_block_diag(x):
    xf = x.astype(F32)
    lane = lax.broadcasted_iota(I32, xf.shape, 1)
    return jnp.concatenate([jnp.where(lane < HEAD_DIM, xf, 0.0), jnp.where(lane >= HEAD_DIM, xf, 0.0)],
                           axis=0).astype(BF16)


def _attn_prompt_kernel(qb_tab, kt_tab, qi_ref, wt_ref, ki2_ref, q_ref, k_ref, vt_ref, o_ref,
                        keys_ref, thr_ref, m_ref, l_ref, acc_ref, bias_ref, cand_ref, dots_ref, vals_ref,
                        qidx_ref, qatt_ref, *, topk):
    i = pl.program_id(0)
    qb = qb_tab[i]
    kt = kt_tab[i]
    q_end = (qb + 1) * QB

    @pl.when(kt == 0)
    def _select():
        nsub = (q_end + SUB - 1) // SUB
        qpos = qb * QB + lax.broadcasted_iota(I32, (1, LANES), 1)
        row_iota = lax.broadcasted_iota(I32, (SUB, LANES), 0)
        for p in range(N_HEADS // 2):
            qidx_ref[p] = _pair_block_diag(qi_ref[:, p * LANES:(p + 1) * LANES])
            qatt_ref[p] = _pair_block_diag(q_ref[:, p * LANES:(p + 1) * LANES])

        def head_dots(s, slot):
            s = jnp.minimum(s, nsub - 1)
            for half in range(2):
                off = half * (SUB // 2)
                kis = ki2_ref[pl.ds(pl.multiple_of(s * SUB + off, SUB // 2), SUB // 2), :]
                for p in range(IDX_HEADS // 2):
                    cols = slice(p * 2 * QB, (p + 1) * 2 * QB)
                    dots_ref[slot * SUB + off:slot * SUB + off + SUB // 2, cols] = _dot_nt(kis, qidx_ref[p])

        def head_sum(s, slot):
            r0 = pl.multiple_of(jnp.minimum(s, nsub - 1) * SUB, SUB)
            acc = jnp.zeros((SUB, LANES), F32)
            for h in range(IDX_HEADS):
                acc = acc + wt_ref[h:h + 1, :] * jnp.maximum(dots_ref[slot * SUB:(slot + 1) * SUB, h * QB:(h + 1) * QB], 0.0)
            acc = jnp.where(acc == 0.0, 0.0, acc)
            sc = jnp.where((r0 + row_iota) <= qpos, acc, NEG)
            vals_ref[pl.ds(r0, SUB), :] = sc
            keys_ref[pl.ds(r0, SUB), :] = _sortable(sc)

        def scores(j, c):
            head_sum(2 * j, 0)
            head_dots(2 * j + 2, 0)
            head_sum(2 * j + 1, 1)
            head_dots(2 * j + 3, 1)
            return c

        head_dots(0, 0)
        head_dots(1, 1)
        lax.fori_loop(0, (nsub + 1) // 2, scores, 0)
        thr = _select_topk(keys_ref, vals_ref, cand_ref, nsub, qpos, topk, (qb * QB) // SUB)
        thr_ref[...] = jnp.broadcast_to(thr, thr_ref.shape)

        def clear(s, c):
            r0 = pl.multiple_of(s * SUB, SUB)
            keys_ref[pl.ds(r0, SUB), :] = jnp.full((SUB, LANES), INT_MIN, I32)
            return c

        lax.fori_loop(nsub, ((q_end + HT - 1) // HT) * (HT // SUB), clear, 0)
        m_ref[...] = jnp.full(m_ref.shape, NEG, F32)
        l_ref[...] = jnp.zeros(l_ref.shape, F32)
        acc_ref[...] = jnp.zeros(acc_ref.shape, F32)

    def attend(j):
        r0 = pl.multiple_of(kt * KT + j * HT, HT)
        bias_ref[...] = jnp.where(keys_ref[pl.ds(r0, HT), :] >= thr_ref[0:1, :], 0.0, 2 * NEG)
        parts = [slice(i * (HT // 2), (i + 1) * (HT // 2)) for i in range(2)]
        rows_of = lambda part: slice(j * HT + part.start, j * HT + part.stop)

        def qk(p):
            return [_dot_nt(k_ref[rows_of(pt), p * LANES:(p + 1) * LANES], qatt_ref[p]) for pt in parts]

        def softmax(h, sts):
            hh = h % 2
            scs = [st[:, hh * QB:(hh + 1) * QB] + bias_ref[pt, :] for st, pt in zip(sts, parts)]
            m_old = m_ref[h:h + 1, :]
            m_new = jnp.maximum(m_old, jnp.maximum(jnp.max(scs[0], axis=0, keepdims=True),
                                                   jnp.max(scs[1], axis=0, keepdims=True)))
            alpha = jnp.exp2(m_old - m_new)
            pms = [jnp.exp2(sc - m_new) for sc in scs]
            l_ref[h:h + 1, :] = (alpha * l_ref[h:h + 1, :] + jnp.sum(pms[0], axis=0, keepdims=True)
                                 + jnp.sum(pms[1], axis=0, keepdims=True))
            m_ref[h:h + 1, :] = m_new
            return alpha, [pm.astype(BF16) for pm in pms]

        def pv(h, alpha, pms):
            rows = slice(h * HEAD_DIM, (h + 1) * HEAD_DIM)
            out = _dot(vt_ref[rows, rows_of(parts[0])], pms[0]) + _dot(vt_ref[rows, rows_of(parts[1])], pms[1])
            acc_ref[rows, :] = alpha * acc_ref[rows, :] + out

        nxt = qk(0)
        pending = None
        for h in range(N_HEADS):
            if h % 2 == 0:
                sts = nxt
                if h + 2 < N_HEADS:
                    nxt = qk(h // 2 + 1)
            cur = (h,) + softmax(h, sts)
            if pending is not None:
                pv(*pending)
            pending = cur
        pv(*pending)

    attend(0)
    for part in range(1, KT // HT):
        pl.when(kt * KT + part * HT < q_end)(functools.partial(attend, part))

    @pl.when((kt + 1) * KT >= q_end)
    def _finish():
        for h in range(N_HEADS):
            rows = slice(h * HEAD_DIM, (h + 1) * HEAD_DIM)
            acc_ref[rows, :] = acc_ref[rows, :] / l_ref[h:h + 1, :]
        o_ref[...] = acc_ref[...].T


def _attn_prompt(qi_b, wi, ki_b, q_b, k_b, v_b, topk):
    t = qi_b.shape[0]
    nb = t // QB
    ki2 = jnp.concatenate([ki_b, ki_b], axis=1)
    wt = wi.T
    vt = v_b.T

    steps = [(b, j) for b in range(nb) for j in range(((b + 1) * QB + KT - 1) // KT)]
    qb_tab = jnp.asarray(np.array([s[0] for s in steps], np.int32))
    kt_tab = jnp.asarray(np.array([s[1] for s in steps], np.int32))
    t_pad = ((t + KT - 1) // KT) * KT

    grid_spec = pltpu.PrefetchScalarGridSpec(
        num_scalar_prefetch=2,
        grid=(len(steps),),
        in_specs=[
            pl.BlockSpec((QB, IDX_HEADS * IDX_DIM), lambda i, qb, kt: (qb[i], 0)),
            pl.BlockSpec((IDX_HEADS, QB), lambda i, qb, kt: (0, qb[i])),
            pl.BlockSpec((t, 2 * IDX_DIM), lambda i, qb, kt: (0, 0)),
            pl.BlockSpec((QB, D_ATTN), lambda i, qb, kt: (qb[i], 0)),
            pl.BlockSpec((KT, D_ATTN), lambda i, qb, kt: (kt[i], 0)),
            pl.BlockSpec((D_ATTN, KT), lambda i, qb, kt: (0, kt[i])),
        ],
        out_specs=pl.BlockSpec((QB, D_ATTN), lambda i, qb, kt: (qb[i], 0)),
        scratch_shapes=[
            pltpu.VMEM((t_pad, LANES), I32),
            pltpu.VMEM((SUBLANES, LANES), I32),
            pltpu.VMEM((N_HEADS, LANES), F32),
            pltpu.VMEM((N_HEADS, LANES), F32),
            pltpu.VMEM((D_ATTN, LANES), F32),
            pltpu.VMEM((HT, LANES), F32),
            pltpu.VMEM((CAND_ROWS, LANES), I32),
            pltpu.VMEM((2 * SUB, IDX_HEADS * QB), F32),
            pltpu.VMEM((t_pad, LANES), F32),
            pltpu.VMEM((IDX_HEADS // 2, 2 * QB, LANES), BF16),
            pltpu.VMEM((N_HEADS // 2, 2 * QB, LANES), BF16),
        ],
    )
    return pl.pallas_call(
        functools.partial(_attn_prompt_kernel, topk=topk),
        grid_spec=grid_spec,
        out_shape=jax.ShapeDtypeStruct((t, D_ATTN), F32),
        compiler_params=_cparams(("arbitrary",)),
        name="attn_prompt",
    )(qb_tab, kt_tab, qi_b, wt, ki2, q_b, k_b, vt)


def _mix_up_kernel(x_ref, cy_ref, ay_ref, gate_ref, shift_ref, scale_ref, gpm_ref, gpf_ref, wo_ref, wu_ref,
                   x1_ref, up_ref):
    mix = _dot(cy_ref[...].astype(BF16), wo_ref[0:D_CONV, :]) + _dot(ay_ref[...].astype(BF16), wo_ref[D_CONV:, :])
    x1 = x_ref[...] + gate_ref[...] * _rms(mix, gpm_ref[...])
    x1_ref[...] = x1
    h2 = (_rms(x1, gpf_ref[...]) * (1.0 + scale_ref[...]) + shift_ref[...]).astype(BF16)
    for c in range(up_ref.shape[1] // 512):
        up_ref[:, c * 512:(c + 1) * 512] = _dot(h2, wu_ref[:, c * 512:(c + 1) * 512])


def _mix_up(x, cy, ay, gate1, shift2, scale2, g_post_mix, g_pre_ffn, w_out, w_up, tm):
    r, d = x.shape
    nf = w_up.shape[1]
    per_row = gate1.shape[0] != 1
    mod_spec = pl.BlockSpec((tm, d), lambda i: (i, 0)) if per_row else pl.BlockSpec((1, d), lambda i: (0, 0))
    row = lambda n: pl.BlockSpec((tm, n), lambda i: (i, 0))
    full = lambda a: pl.BlockSpec(a.shape, lambda i: (0, 0))
    return pl.pallas_call(
        _mix_up_kernel,
        grid=(r // tm,),
        in_specs=[row(d), row(D_CONV), row(D_ATTN), mod_spec, mod_spec, mod_spec, full(g_post_mix), full(g_pre_ffn),
                  full(w_out), full(w_up)],
        out_specs=[row(d), row(nf)],
        out_shape=[jax.ShapeDtypeStruct((r, d), F32), jax.ShapeDtypeStruct((r, nf), F32)],
        compiler_params=_cparams(("arbitrary",)),
        name="mix_up",
    )(x, cy, ay, gate1, shift2, scale2, g_post_mix, g_pre_ffn, w_out, w_up)


def _ffn_down_kernel(*refs, bs, tm, hb, multi):
    if multi:
        hist_ref, halo_ref, main_ref, x1_ref, gate_ref, w_ref, b_ref, wd_ref, g_ref, o_ref, win_ref = refs
    else:
        hist_ref, main_ref, x1_ref, gate_ref, w_ref, b_ref, wd_ref, g_ref, o_ref, win_ref = refs
        halo_ref = None
    _fill_window(win_ref, hist_ref, halo_ref, main_ref, hb, tm)
    taps = w_ref.shape[0]
    base = hb - (taps - 1) * bs
    cw = 256

    def conv(c0):
        acc = jnp.broadcast_to(b_ref[:, c0:c0 + cw], (tm, cw))
        for j in range(taps):
            acc = acc + w_ref[j:j + 1, c0:c0 + cw] * win_ref[base + j * bs:base + j * bs + tm, c0:c0 + cw]
        return acc

    f = jnp.zeros((tm, o_ref.shape[1]), F32)
    for c in range(D_FF // cw):
        a = conv(c * cw)
        g = conv(D_FF + c * cw)
        gated = (g * jax.nn.sigmoid(g) * a).astype(BF16)
        f = f + _dot(gated, wd_ref[c * cw:(c + 1) * cw, :])
    o_ref[...] = x1_ref[...] + gate_ref[...] * _rms(f, g_ref[...])


def _ffn_down(hist, up, x1, gate2, w, b, w_down, g, *, bs, tm):
    r, nf = up.shape
    d = x1.shape[1]
    hb = hist.shape[0]
    n_tiles = r // tm
    per_row = gate2.shape[0] != 1
    mod_spec = pl.BlockSpec((tm, d), lambda i: (i, 0)) if per_row else pl.BlockSpec((1, d), lambda i: (0, 0))
    full = lambda a: pl.BlockSpec(a.shape, lambda i: (0, 0))
    kern = functools.partial(_ffn_down_kernel, bs=bs, tm=tm, hb=hb, multi=n_tiles > 1)
    args = [hist] + ([up] if n_tiles > 1 else []) + [up, x1, gate2, w, b, w_down, g]
    return pl.pallas_call(
        kern,
        grid=(n_tiles,),
        in_specs=_window_specs(hist, nf, tm, hb, n_tiles) + [pl.BlockSpec((tm, d), lambda i: (i, 0)), mod_spec,
                                                              full(w), full(b), full(w_down), full(g)],
        out_specs=pl.BlockSpec((tm, d), lambda i: (i, 0)),
        out_shape=jax.ShapeDtypeStruct((r, d), F32),
        scratch_shapes=[pltpu.VMEM((hb + tm, nf), F32)],
        compiler_params=_cparams(("arbitrary",)),
        name="ffn_down",
    )(*args)


def _sample_scores_kernel(pt_ref, q_ref, w_ref, knew_ref, *refs, n_chunks, past, n_new):
    page_refs, o_ref = refs[:PPC], refs[PPC]
    c = pl.program_id(1)
    q = q_ref[...]
    w = w_ref[...]

    def head_sum(page):
        sc = _dot(q, page.astype(BF16))
        val = w * jnp.maximum(sc, 0.0)
        tot = val.reshape(IDX_HEADS, SUBLANES, LANES).sum(axis=0)
        return jnp.where(tot == 0.0, 0.0, tot)

    tq = lax.broadcasted_iota(I32, (SUBLANES, LANES), 0)
    j = lax.broadcasted_iota(I32, (SUBLANES, LANES), 1)

    @pl.when(c < n_chunks)
    def _past():
        for r in range(PPC):
            pos = (c * PPC + r) * PAGE_SIZE + j
            o_ref[:, r * LANES:(r + 1) * LANES] = jnp.where(tq < n_new, head_sum(page_refs[r][...]), -pos.astype(F32))

    @pl.when(c == n_chunks)
    def _new():
        valid = jnp.logical_and(j <= tq, j < n_new)
        o_ref[:, 0:LANES] = jnp.where(valid, head_sum(knew_ref[...]), NEG)
        o_ref[:, LANES:] = jnp.full((SUBLANES, (PPC - 1) * LANES), NEG, F32)


def _sample_scores(page_table, q_rows, w_rows, kit_new, cache_kit, layer):
    nb, n_pages = page_table.shape
    n_chunks = n_pages // PPC
    past = n_pages * PAGE_SIZE
    cw = PPC * PAGE_SIZE

    def page_spec(r):
        return pl.BlockSpec((None, None, IDX_DIM, PAGE_SIZE),
                            lambda b, c, pt: (layer, pt[b, jnp.minimum(c * PPC + r, n_pages - 1)], 0, 0))

    grid_spec = pltpu.PrefetchScalarGridSpec(
        num_scalar_prefetch=1,
        grid=(nb, n_chunks + 1),
        in_specs=[pl.BlockSpec((None, IDX_HEADS * SUBLANES, IDX_DIM), lambda b, c, pt: (b, 0, 0)),
                  pl.BlockSpec((None, IDX_HEADS * SUBLANES, LANES), lambda b, c, pt: (b, 0, 0)),
                  pl.BlockSpec((None, IDX_DIM, PAGE_SIZE), lambda b, c, pt: (b, 0, 0))]
                 + [page_spec(r) for r in range(PPC)],
        out_specs=pl.BlockSpec((None, SUBLANES, cw), lambda b, c, pt: (b, 0, c)),
    )
    return pl.pallas_call(
        functools.partial(_sample_scores_kernel, n_chunks=n_chunks, past=past, n_new=4),
        grid_spec=grid_spec,
        out_shape=jax.ShapeDtypeStruct((nb, SUBLANES, past + cw), F32),
        compiler_params=_cparams(("arbitrary", "arbitrary")),
        name="sample_scores",
    )(page_table, q_rows, w_rows, kit_new, *([cache_kit] * PPC))


def _sample_select_kernel(sc_ref, lim_ref, keys_ref, thr_ref, cand_ref, *, topk):
    rows = sc_ref.shape[0]

    def conv(s, c):
        r0 = pl.multiple_of(s * SUB, SUB)
        keys_ref[pl.ds(r0, SUB), :] = _sortable(sc_ref[pl.ds(r0, SUB), :])
        return c

    lax.fori_loop(0, rows // SUB, conv, 0)
    thr = _select_topk(keys_ref, None, cand_ref, rows // SUB, lim_ref[...], topk, 0)
    thr_ref[...] = jnp.broadcast_to(thr, thr_ref.shape)


def _sample_select(scores_t, limit, topk):
    rows, nq = scores_t.shape
    return pl.pallas_call(
        functools.partial(_sample_select_kernel, topk=topk),
        grid=(nq // LANES,),
        in_specs=[pl.BlockSpec((rows, LANES), lambda g: (0, g)), pl.BlockSpec((1, LANES), lambda g: (0, g))],
        out_specs=[pl.BlockSpec((rows, LANES), lambda g: (0, g)), pl.BlockSpec((SUBLANES, LANES), lambda g: (0, g))],
        out_shape=[jax.ShapeDtypeStruct((rows, nq), I32), jax.ShapeDtypeStruct((SUBLANES, nq), I32)],
        scratch_shapes=[pltpu.VMEM((CAND_ROWS, LANES), I32)],
        compiler_params=_cparams(("arbitrary",)),
        name="sample_select",
    )(scores_t, limit)


def _sample_attn_kernel(pt_ref, q_ref, keys_ref, thr_ref, knew_ref, vnew_ref, *refs, n_chunks):
    k_refs, v_refs = refs[:APC], refs[APC:2 * APC]
    o_ref, m_ref, l_ref, acc_ref = refs[2 * APC:]
    c = pl.program_id(1)

    @pl.when(c == 0)
    def _init():
        m_ref[...] = jnp.full(m_ref.shape, NEG, F32)
        l_ref[...] = jnp.zeros(l_ref.shape, F32)
        acc_ref[...] = jnp.zeros(acc_ref.shape, F32)

    thr = thr_ref[...]

    def pages_update(pages):
        n = len(pages)
        bias = jnp.where(keys_ref[:, 0:n * LANES] >= thr[:, 0:1], 0.0, 2 * NEG)
        scs = []
        for h in range(N_HEADS):
            kt = jnp.concatenate([kp[h].astype(BF16) for kp, _ in pages], axis=1)
            scs.append(_dot(q_ref[h * SUBLANES:(h + 1) * SUBLANES, :], kt) + bias)
        sc = jnp.concatenate(scs, axis=0)
        m_old = m_ref[...]
        m_new = jnp.maximum(m_old, jnp.max(sc, axis=1, keepdims=True))
        alpha = jnp.exp2(m_old - m_new)
        pm = jnp.exp2(sc - m_new)
        l_ref[...] = alpha * l_ref[...] + jnp.sum(pm, axis=1, keepdims=True)
        pb = pm.astype(BF16)
        outs = []
        for h in range(N_HEADS):
            vt = jnp.concatenate([vp[h].astype(BF16) for _, vp in pages], axis=1)
            outs.append(_dot_nt(pb[h * SUBLANES:(h + 1) * SUBLANES, :], vt))
        acc_ref[...] = alpha * acc_ref[...] + jnp.concatenate(outs, axis=0)
        m_ref[...] = m_new

    @pl.when(c < n_chunks)
    def _past():
        pages_update(list(zip(k_refs, v_refs)))

    @pl.when(c == n_chunks)
    def _new():
        pages_update([(knew_ref, vnew_ref)])
        out = acc_ref[...] / l_ref[...]
        o_ref[...] = jnp.concatenate([out[h * SUBLANES:(h + 1) * SUBLANES, :] for h in range(N_HEADS)], axis=1)


def _sample_attn(page_table, q_rows, keys, thr, kt_new, vt_new, cache_kt, cache_vt, layer):
    nb, n_pages = page_table.shape
    n_chunks = n_pages // APC
    cw = APC * PAGE_SIZE
    rows = N_HEADS * SUBLANES

    def page_spec(r):
        return pl.BlockSpec((None, None, N_HEADS, HEAD_DIM, PAGE_SIZE),
                            lambda b, c, pt: (layer, pt[b, jnp.minimum(c * APC + r, n_pages - 1)], 0, 0, 0))

    new_spec = pl.BlockSpec((None, N_HEADS, HEAD_DIM, PAGE_SIZE), lambda b, c, pt: (b, 0, 0, 0))
    grid_spec = pltpu.PrefetchScalarGridSpec(
        num_scalar_prefetch=1,
        grid=(nb, n_chunks + 1),
        in_specs=[pl.BlockSpec((None, rows, HEAD_DIM), lambda b, c, pt: (b, 0, 0)),
                  pl.BlockSpec((None, SUBLANES, cw), lambda b, c, pt: (b, 0, c)),
                  pl.BlockSpec((None, SUBLANES, LANES), lambda b, c, pt: (b, 0, 0)),
                  new_spec, new_spec]
                 + [page_spec(r) for r in range(APC)] * 2,
        out_specs=pl.BlockSpec((None, SUBLANES, D_ATTN), lambda b, c, pt: (b, 0, 0)),
        scratch_shapes=[pltpu.VMEM((rows, 1), F32), pltpu.VMEM((rows, 1), F32), pltpu.VMEM((rows, HEAD_DIM), F32)],
    )
    return pl.pallas_call(
        functools.partial(_sample_attn_kernel, n_chunks=n_chunks),
        grid_spec=grid_spec,
        out_shape=jax.ShapeDtypeStruct((nb, SUBLANES, D_ATTN), F32),
        compiler_params=_cparams(("arbitrary", "arbitrary")),
        name="sample_attn",
    )(page_table, q_rows, keys, thr, kt_new, vt_new, *([cache_kt] * APC), *([cache_vt] * APC))


def _split_mod(mod):
    return [mod[:, i * D_MODEL:(i + 1) * D_MODEL] for i in range(6)]


def _layer_prompt(x, mod, wts):
    t = x.shape[0]
    shift1, scale1, gate1, shift2, scale2, gate2 = _split_mod(mod)
    glu, q_b, k, k_b, v, v_b, qi_b, tail = _inproj(x, shift1, scale1, wts["g_pre_mix"], wts["w_main"], wts["w_tail"], 512)
    ki = tail[:, :IDX_DIM]
    wi = tail[:, IDX_DIM:IDX_DIM + IDX_HEADS]
    conv_hist = jnp.zeros((32, D_CONV), F32)
    conv_y = _conv_ln(conv_hist, glu, wts["w_dw"], wts["b_dw"], wts["ln_g"], wts["ln_b"], bs=1, tm=256, rc=32)
    attn_y = _attn_prompt(qi_b, wi, ki.astype(BF16), q_b, k_b, v_b, min(TOPK_MAX, t // 4))
    x1, up = _mix_up(x, conv_y, attn_y, gate1, shift2, scale2, wts["g_post_mix"], wts["g_pre_ffn"],
                     wts["w_out"], wts["w_up"], 512)
    ffn_hist = jnp.zeros((SUBLANES, 2 * D_FF), F32)
    y = _ffn_down(ffn_hist, up, x1, gate2, wts["w_ffn_dw"], wts["b_ffn_dw"], wts["w_down"], wts["g_post_ffn"],
                  bs=1, tm=256)
    return y, k, v, ki, glu[t - (CONV_WIDTH - 1):], up[t - (FFN_CONV_WIDTH - 1):]


def _layer_sample(x, mod, conv_state, ffn_state, cache_kt, cache_vt, cache_kit, page_table, layer, wts):
    nb, n_pages = page_table.shape
    r = x.shape[0]
    nt = r // nb
    past = n_pages * PAGE_SIZE
    shift1, scale1, gate1, shift2, scale2, gate2 = [jnp.tile(m, (nt, 1)) for m in _split_mod(mod)]
    glu, q_b, k, k_b, v, v_b, qi_b, tail = _inproj(x, shift1, scale1, wts["g_pre_mix"], wts["w_main"], wts["w_tail"], r)
    ki = tail[:, :IDX_DIM]
    wi = tail[:, IDX_DIM:IDX_DIM + IDX_HEADS]
    conv_hist = conv_state.transpose(1, 0, 2).reshape((CONV_WIDTH - 1) * nb, D_CONV)
    conv_y = _conv_ln(conv_hist, glu, wts["w_dw"], wts["b_dw"], wts["ln_g"], wts["ln_b"], bs=nb, tm=r, rc=32)

    tb = lambda a: a.reshape(nt, nb, -1).transpose(1, 0, 2)
    pad_t = lambda a: jnp.pad(a, ((0, 0), (0, SUBLANES - nt)) + ((0, 0),) * (a.ndim - 2))
    qi_s = pad_t(tb(qi_b).reshape(nb, nt, IDX_HEADS, IDX_DIM)).transpose(0, 2, 1, 3)
    q_rows = qi_s.reshape(nb, IDX_HEADS * SUBLANES, IDX_DIM)
    w_s = pad_t(tb(wi)).transpose(0, 2, 1).reshape(nb, IDX_HEADS * SUBLANES, 1)
    w_rows = jnp.broadcast_to(w_s, (nb, IDX_HEADS * SUBLANES, LANES))
    pad_keys = lambda a: jnp.pad(tb(a), ((0, 0), (0, PAGE_SIZE - nt), (0, 0)))
    kit_new = pad_keys(ki).transpose(0, 2, 1)
    scores = _sample_scores(page_table, q_rows, w_rows, kit_new, cache_kit, layer)
    total = scores.shape[2]
    topk = min(TOPK_MAX, (past + nt) // 4)
    scores_t = scores.reshape(nb * SUBLANES, total).T
    limit = jnp.tile(past + jnp.arange(SUBLANES, dtype=I32), nb).reshape(1, nb * SUBLANES)
    keys_t, thr = _sample_select(scores_t, limit, topk)
    keys = keys_t.T.reshape(nb, SUBLANES, total)
    thr_rows = jnp.broadcast_to(thr[0].reshape(nb, SUBLANES, 1), (nb, SUBLANES, LANES))

    q_s = pad_t(tb(q_b).reshape(nb, nt, N_HEADS, HEAD_DIM)).transpose(0, 2, 1, 3)
    new_t = lambda a: pad_keys(a).reshape(nb, PAGE_SIZE, N_HEADS, HEAD_DIM).transpose(0, 2, 3, 1)
    attn = _sample_attn(page_table, q_s.reshape(nb, N_HEADS * SUBLANES, HEAD_DIM), keys, thr_rows,
                        new_t(k), new_t(v), cache_kt, cache_vt, layer)
    attn_y = attn[:, :nt].transpose(1, 0, 2).reshape(r, D_ATTN)

    x1, up = _mix_up(x, conv_y, attn_y, gate1, shift2, scale2, wts["g_post_mix"], wts["g_pre_ffn"],
                     wts["w_out"], wts["w_up"], r)
    ffn_hist = ffn_state.transpose(1, 0, 2).reshape((FFN_CONV_WIDTH - 1) * nb, 2 * D_FF)
    y = _ffn_down(ffn_hist, up, x1, gate2, wts["w_ffn_dw"], wts["b_ffn_dw"], wts["w_down"], wts["g_post_ffn"],
                  bs=nb, tm=r)
    conv_all = jnp.concatenate([conv_hist, glu], axis=0).reshape(CONV_WIDTH - 1 + nt, nb, D_CONV)
    ffn_all = jnp.concatenate([ffn_hist, up], axis=0).reshape(FFN_CONV_WIDTH - 1 + nt, nb, 2 * D_FF)
    return (y, k, v, ki, conv_all[nt:].transpose(1, 0, 2), ffn_all[nt:].transpose(1, 0, 2))


def kernel(x_prompt, x_sample, cache_k, cache_v, cache_kidx, state_conv, state_ffn, page_table, c_prompt, c_sample, w_ada, b_ada, g_pre_mix, w_in, w_dw, b_dw, ln_g, ln_b, w_out, g_post_mix, g_pre_ffn, w_up, w_ffn_dw, b_ffn_dw, w_down, g_post_ffn):
    depth = w_ada.shape[0]
    pb, seq, d = x_prompt.shape
    db, dt, _ = x_sample.shape
    assert pb == 1 and d == D_MODEL and seq % KT == 0 and db * dt == LANES and dt <= SUBLANES
    cache_kt = cache_k.transpose(0, 1, 3, 4, 2)
    cache_vt = cache_v.transpose(0, 1, 3, 4, 2)
    cache_kit = cache_kidx.transpose(0, 1, 3, 2)

    xp = x_prompt[0]
    xs = x_sample.transpose(1, 0, 2).reshape(dt * db, d)
    n_c = pb + db
    c_all = jnp.pad(jnp.concatenate([c_prompt, c_sample], axis=0), ((0, (-n_c) % SUBLANES), (0, 0)))
    row = lambda a: a.reshape(1, -1)
    outs_p, outs_s = [], []
    for l in range(depth):
        mod = _ada(c_all, w_ada[l], b_ada[l])
        wts = dict(
            g_pre_mix=row(g_pre_mix[l]), w_main=w_in[l][:, :N_MAIN].astype(BF16),
            w_tail=jnp.pad(w_in[l][:, N_MAIN:], ((0, 0), (0, LANES - IDX_DIM - IDX_HEADS))).astype(BF16),
            w_dw=w_dw[l], b_dw=row(b_dw[l]), ln_g=row(ln_g[l]), ln_b=row(ln_b[l]),
            w_out=w_out[l].astype(BF16), g_post_mix=row(g_post_mix[l]), g_pre_ffn=row(g_pre_ffn[l]),
            w_up=w_up[l].astype(BF16), w_ffn_dw=w_ffn_dw[l], b_ffn_dw=row(b_ffn_dw[l]),
            w_down=w_down[l].astype(BF16), g_post_ffn=row(g_post_ffn[l]))
        xp, kp, vp, kip, cp, fp = _layer_prompt(xp, mod[:pb], wts)
        xs, k_s, v_s, ki_s, c_s, f_s = _layer_sample(xs, mod[pb:n_c], state_conv[l], state_ffn[l], cache_kt, cache_vt,
                                                      cache_kit, page_table, l, wts)
        unflat = lambda a, n: a.reshape(dt, db, n).transpose(1, 0, 2)
        outs_p.append((kp.reshape(pb, seq, N_HEADS, HEAD_DIM), vp.reshape(pb, seq, N_HEADS, HEAD_DIM),
                       kip.reshape(pb, seq, IDX_DIM), cp[None], fp[None]))
        outs_s.append((unflat(k_s, D_ATTN).reshape(db, dt, N_HEADS, HEAD_DIM),
                       unflat(v_s, D_ATTN).reshape(db, dt, N_HEADS, HEAD_DIM), unflat(ki_s, IDX_DIM), c_s, f_s))
    stack = lambda outs, i: jnp.stack([o[i] for o in outs])
    y_prompt = xp[None]
    y_sample = xs.reshape(dt, db, d).transpose(1, 0, 2)
    return (y_prompt, y_sample) + tuple(stack(outs_p, i) for i in range(5)) + tuple(stack(outs_s, i) for i in range(5))
```

```python
import functools

import jax
import jax.numpy as jnp
import numpy as np
from jax import lax
from jax.experimental import pallas as pl
from jax.experimental.pallas import tpu as pltpu

F32 = jnp.float32
BF16 = jnp.bfloat16
I32 = jnp.int32

D_MODEL = 1024
D_CONV = 512
CONV_WIDTH = 31
N_HEADS = 8
HEAD_DIM = 64
D_ATTN = N_HEADS * HEAD_DIM
IDX_HEADS = 8
IDX_DIM = 64
TOPK_MAX = 256
D_FF = 2816
FFN_CONV_WIDTH = 3
PAGE_SIZE = 128
EPS = 1e-6
NEG = -1e30
INDEX_WEIGHT_SCALE = (IDX_HEADS * IDX_DIM) ** -0.5
LOG2E = 1.4426950408889634
N_MAIN = 6 * 512
INT_MIN = -(2 ** 31)

LANES = 128
SUBLANES = 8
VMEM_LIMIT = 56 * 1024 * 1024

QB = 128
SUB = 256
HT = 1024
KT = HT
PPC = 16
APC = 16
SLAB = 128
N_PHASE = SLAB // SUBLANES
KEEP = 12
CAND_ROWS = N_PHASE * KEEP * SUBLANES


def _cparams(sem):
    return pltpu.CompilerParams(dimension_semantics=sem, vmem_limit_bytes=VMEM_LIMIT)


def _rms(x, g):
    return x * lax.rsqrt(jnp.mean(x * x, axis=-1, keepdims=True) + EPS) * g


def _dot(a, b):
    return jnp.dot(a, b, preferred_element_type=F32)


def _dot_nt(a, b):
    return lax.dot_general(a, b, (((1,), (1,)), ((), ())), preferred_element_type=F32)


def _ada_kernel(c_ref, w_ref, b_ref, o_ref):
    c = c_ref[...]
    s = (c * jax.nn.sigmoid(c)).astype(BF16)
    o_ref[...] = _dot(s, w_ref[...].astype(BF16)) + b_ref[...]


def _ada(c_all, w_ada, b_ada):
    r, d = c_all.shape
    n = w_ada.shape[1]
    tn = 1024
    return pl.pallas_call(
        _ada_kernel,
        grid=(n // tn,),
        in_specs=[pl.BlockSpec((r, d), lambda j: (0, 0)),
                  pl.BlockSpec((d, tn), lambda j: (0, j)),
                  pl.BlockSpec((1, tn), lambda j: (0, j))],
        out_specs=pl.BlockSpec((r, tn), lambda j: (0, j)),
        out_shape=jax.ShapeDtypeStruct((r, n), F32),
        compiler_params=_cparams(("arbitrary",)),
        name="ada",
    )(c_all, w_ada, b_ada.reshape(1, n))


def _inproj_kernel(x_ref, shift_ref, scale_ref, g_ref, wm_ref, wt_ref,
                   glu_ref, q_ref, k_ref, kb_ref, v_ref, vb_ref, qi_ref, tail_ref):
    h = _rms(x_ref[...], g_ref[...]) * (1.0 + scale_ref[...]) + shift_ref[...]
    hb = h.astype(BF16)

    def proj(i):
        return _dot(hb, wm_ref[:, i * 512:(i + 1) * 512])

    ca = proj(0)
    cg = proj(1)
    glu_ref[...] = ca * jax.nn.sigmoid(cg)
    q_ref[...] = (proj(2) * (HEAD_DIM ** -0.5 * LOG2E)).astype(BF16)
    k = proj(3)
    k_ref[...] = k
    kb_ref[...] = k.astype(BF16)
    v = proj(4)
    v_ref[...] = v
    vb_ref[...] = v.astype(BF16)
    qi_ref[...] = proj(5).astype(BF16)
    lane = lax.broadcasted_iota(I32, (1, LANES), 1)
    tail_ref[...] = _dot(hb, wt_ref[...]) * jnp.where(lane >= IDX_DIM, INDEX_WEIGHT_SCALE, 1.0)


def _inproj(x, shift, scale, g, w_main, w_tail, tm):
    r, d = x.shape
    per_row = shift.shape[0] != 1
    mod_spec = pl.BlockSpec((tm, d), lambda i: (i, 0)) if per_row else pl.BlockSpec((1, d), lambda i: (0, 0))
    row = lambda n: pl.BlockSpec((tm, n), lambda i: (i, 0))
    full = lambda a: pl.BlockSpec(a.shape, lambda i: (0, 0))
    sds = lambda n, dt: jax.ShapeDtypeStruct((r, n), dt)
    return pl.pallas_call(
        _inproj_kernel,
        grid=(r // tm,),
        in_specs=[row(d), mod_spec, mod_spec, full(g), full(w_main), full(w_tail)],
        out_specs=[row(512)] * 7 + [row(LANES)],
        out_shape=[sds(512, F32), sds(512, BF16), sds(512, F32), sds(512, BF16), sds(512, F32), sds(512, BF16),
                   sds(512, BF16), sds(LANES, F32)],
        compiler_params=_cparams(("arbitrary",)),
        name="inproj",
    )(x, shift, scale, g, w_main, w_tail)


def _fill_window(win_ref, hist_ref, halo_ref, main_ref, hb, tm):
    win_ref[hb:hb + tm, :] = main_ref[...]
    if halo_ref is None:
        win_ref[0:hb, :] = hist_ref[...]
    else:
        first = pl.program_id(0) == 0

        @pl.when(first)
        def _():
            win_ref[0:hb, :] = hist_ref[...]

        @pl.when(jnp.logical_not(first))
        def _():
            win_ref[0:hb, :] = halo_ref[...]


def _window_specs(hist, main_cols, tm, hb, n_tiles):
    specs = [pl.BlockSpec(hist.shape, lambda i: (0, 0))]
    if n_tiles > 1:
        per = tm // hb
        specs.append(pl.BlockSpec((hb, main_cols), lambda i: (jnp.maximum(i * per - 1, 0), 0)))
    specs.append(pl.BlockSpec((tm, main_cols), lambda i: (i, 0)))
    return specs


def _tap_residues(taps, bs, hb):
    base = hb - (taps - 1) * bs
    return sorted({(base + j * bs) % SUBLANES for j in range(taps)} - {0})


def _conv_ln_kernel(*refs, taps, bs, tm, hb, rc, multi):
    residues = _tap_residues(taps, bs, hb)
    refs = list(refs)
    sh_ref = refs.pop() if residues else None
    if multi:
        hist_ref, halo_ref, main_ref, w_ref, b_ref, g_ref, beta_ref, o_ref, win_ref = refs
    else:
        hist_ref, main_ref, w_ref, b_ref, g_ref, beta_ref, o_ref, win_ref = refs
        halo_ref = None
    _fill_window(win_ref, hist_ref, halo_ref, main_ref, hb, tm)
    base = hb - (taps - 1) * bs
    c = o_ref.shape[1]
    span = hb + tm - SUBLANES
    for rho in residues:
        sh_ref[rho, 0:span, :] = win_ref[rho:rho + span, :]

    def rows_at(r0):
        rho = r0 % SUBLANES
        return win_ref[r0:r0 + rc, :] if rho == 0 else sh_ref[rho, r0 - rho:r0 - rho + rc, :]

    for ci in range(tm // rc):
        acc = jnp.broadcast_to(b_ref[...], (rc, c))
        for j in range(taps):
            acc = acc + w_ref[j:j + 1, :] * rows_at(base + ci * rc + j * bs)
        mu = jnp.mean(acc, axis=-1, keepdims=True)
        xc = acc - mu
        var = jnp.mean(xc * xc, axis=-1, keepdims=True)
        y = xc * lax.rsqrt(var + EPS) * g_ref[...] + beta_ref[...]
        o_ref[ci * rc:(ci + 1) * rc, :] = y * jax.nn.sigmoid(y)


def _conv_ln(hist, glu, w, b, g, beta, *, bs, tm, rc):
    r, c = glu.shape
    taps = w.shape[0]
    hb = hist.shape[0]
    n_tiles = r // tm
    vec = lambda a: pl.BlockSpec(a.shape, lambda i: (0, 0))
    kern = functools.partial(_conv_ln_kernel, taps=taps, bs=bs, tm=tm, hb=hb, rc=rc, multi=n_tiles > 1)
    args = [hist] + ([glu] if n_tiles > 1 else []) + [glu, w, b, g, beta]
    return pl.pallas_call(
        kern,
        grid=(n_tiles,),
        in_specs=_window_specs(hist, c, tm, hb, n_tiles) + [vec(w), vec(b), vec(g), vec(beta)],
        out_specs=pl.BlockSpec((tm, c), lambda i: (i, 0)),
        out_shape=jax.ShapeDtypeStruct((r, c), F32),
        scratch_shapes=[pltpu.VMEM((hb + tm, c), F32)]
                       + ([pltpu.VMEM((SUBLANES, hb + tm, c), F32)] if _tap_residues(taps, bs, hb) else []),
        compiler_params=_cparams(("arbitrary",)),
        name="conv_ln",
    )(*args)


def _sortable(x):
    bits = pltpu.bitcast(x, I32)
    return bits ^ ((bits >> 31) & 0x7FFFFFFF)


def _select_topk(keys_ref, vals_ref, cand_ref, nsub, limit, topk, demote_from):
    zero8 = jnp.zeros((SUBLANES, LANES), I32)

    def count_in(ref, n, pred):
        def body(s, acc):
            r0 = s * SUB if isinstance(s, int) else pl.multiple_of(s * SUB, SUB)
            kk = ref[pl.ds(r0, SUB), :]
            hit = jnp.where(pred(kk, r0), 1, 0)
            return acc + hit.reshape(SUB // SUBLANES, SUBLANES, LANES).sum(axis=0)
        if isinstance(n, int):
            acc = functools.reduce(lambda a, s: body(s, a), range(n), zero8)
        else:
            acc = lax.fori_loop(0, n, body, zero8)
        return jnp.sum(acc, axis=0, keepdims=True)

    def count(pred):
        return count_in(keys_ref, nsub, pred)

    def kth_largest(ref, n):
        ge = lambda cand_u: count_in(ref, n, lambda kk, r0: kk >= (cand_u ^ INT_MIN))

        def bit_body(b, tau_u):
            cand = tau_u | jnp.left_shift(jnp.int32(1), 31 - b)
            return jnp.where(ge(cand) >= topk, cand, tau_u)

        return lax.fori_loop(0, 32, bit_body, jnp.zeros((1, LANES), I32)) ^ INT_MIN

    def prefiltered():
        src_ref = keys_ref if vals_ref is None else vals_ref
        lowest = INT_MIN if vals_ref is None else -jnp.inf
        to_key = (lambda v: v) if vals_ref is None else _sortable
        worst = jnp.full((SUBLANES, LANES), INT_MIN, I32)
        for ph2 in range(N_PHASE // 2):
            def body(s, ls):
                ls = list(ls)
                r0 = pl.multiple_of(s * SUB, SUB)
                for slab in range(SUB // SLAB):
                    for k in range(2):
                        x = src_ref[pl.ds(r0 + slab * SLAB + (2 * ph2 + k) * SUBLANES, SUBLANES), :]
                        for i in range(KEEP):
                            cur = ls[k * KEEP + i]
                            if vals_ref is None:
                                up = x > cur
                                ls[k * KEEP + i] = jnp.where(up, x, cur)
                                x = jnp.where(up, cur, x)
                            else:
                                ls[k * KEEP + i] = jnp.maximum(cur, x)
                                x = jnp.minimum(cur, x)
                return tuple(ls)

            init = tuple(jnp.full((SUBLANES, LANES), lowest, src_ref.dtype) for _ in range(2 * KEEP))
            ls = [to_key(v) for v in lax.fori_loop(0, nsub, body, init)]
            for k in range(2):
                for i in range(KEEP):
                    c0 = ((2 * ph2 + k) * KEEP + i) * SUBLANES
                    cand_ref[c0:c0 + SUBLANES, :] = ls[k * KEEP + i]
                worst = jnp.maximum(worst, ls[k * KEEP + KEEP - 1])
        n_cand = CAND_ROWS // SUB
        tau = kth_largest(cand_ref, n_cand)
        unsafe = jnp.max(worst, axis=0, keepdims=True) >= tau
        return tau, jnp.max(jnp.where(unsafe, 1, 0)), count_in(cand_ref, n_cand, lambda kk, r0: kk >= tau)

    def exact():
        tau = kth_largest(keys_ref, nsub)
        return tau, count(lambda kk, r0: kk >= tau)

    zero = jnp.zeros((1, LANES), I32)
    tau, unsafe, n_ge = lax.cond(nsub * (SUB // SLAB) > KEEP, prefiltered, lambda: (zero, jnp.int32(1), zero))
    tau, n_ge = lax.cond(unsafe > 0, exact, lambda: (tau, n_ge))

    row_iota = lax.broadcasted_iota(I32, (SUB, LANES), 0)

    @pl.when(jnp.max(n_ge) > topk)
    def _ties():
        n_gt = count(lambda kk, r0: kk > tau)
        keep = topk - n_gt

        def pos_body(b, q):
            cand = q | jnp.left_shift(jnp.int32(1), 30 - b)
            n = count(lambda kk, r0: jnp.logical_and(kk == tau, (r0 + row_iota) < cand))
            return jnp.where(n < keep, cand, q)

        last = lax.fori_loop(0, 31, pos_body, jnp.zeros((1, LANES), I32))
        lower = jnp.where(tau == INT_MIN, INT_MIN, tau - 1)

        def demote(s, c):
            r0 = pl.multiple_of(s * SUB, SUB)
            kk = keys_ref[pl.ds(r0, SUB), :]
            drop = jnp.logical_and(kk == tau, (r0 + row_iota) > last)
            keys_ref[pl.ds(r0, SUB), :] = jnp.where(drop, lower, kk)
            return c

        lax.fori_loop(0, nsub, demote, 0)

    def causal(s, c):
        r0 = pl.multiple_of(s * SUB, SUB)
        kk = keys_ref[pl.ds(r0, SUB), :]
        keys_ref[pl.ds(r0, SUB), :] = jnp.where((r0 + row_iota) <= limit, kk, INT_MIN)
        return c

    lax.fori_loop(demote_from, nsub, causal, 0)
    return jnp.maximum(tau, INT_MIN + 1)


def _pair_block_diag(x):
    xf = x.astype(F32)
    lane = lax.broadcasted_iota(I32, xf.shape, 1)
    return jnp.concatenate([jnp.where(lane < HEAD_DIM, xf, 0.0), jnp.where(lane >= HEAD_DIM, xf, 0.0)],
                           axis=0).astype(BF16)


def _attn_prompt_kernel(qb_tab, kt_tab, qi_ref, wt_ref, ki_ref, q_ref, k_ref, vt_ref, o_ref,
                        keys_ref, thr_ref, m_ref, l_ref, acc_ref, bias_ref, cand_ref, dots_ref, vals_ref,
                        qidx_ref, qatt_ref, *, topk):
    i = pl.program_id(0)
    qb = qb_tab[i]
    kt = kt_tab[i]
    q_end = (qb + 1) * QB

    @pl.when(kt == 0)
    def _select():
        nsub = (q_end + SUB - 1) // SUB
        qpos = qb * QB + lax.broadcasted_iota(I32, (1, LANES), 1)
        row_iota = lax.broadcasted_iota(I32, (SUB, LANES), 0)
        for p in range(N_HEADS // 2):
            xi = qi_ref[:, p * LANES:(p + 1) * LANES].astype(F32)
            second = pltpu.roll(xi, IDX_DIM, axis=1)
            qidx_ref[p] = jnp.concatenate([xi[:, :IDX_DIM], second[:, :IDX_DIM]], axis=0).astype(BF16)
            qatt_ref[p] = _pair_block_diag(q_ref[:, p * LANES:(p + 1) * LANES])

        def head_dots(s, slot):
            s = jnp.minimum(s, nsub - 1)
            for half in range(2):
                off = half * (SUB // 2)
                kis = ki_ref[pl.ds(pl.multiple_of(s * SUB + off, SUB // 2), SUB // 2), :]
                for p in range(IDX_HEADS // 2):
                    cols = slice(p * 2 * QB, (p + 1) * 2 * QB)
                    dots_ref[slot * SUB + off:slot * SUB + off + SUB // 2, cols] = _dot_nt(kis, qidx_ref[p])

        def head_sum(s, slot):
            r0 = pl.multiple_of(jnp.minimum(s, nsub - 1) * SUB, SUB)
            acc = jnp.zeros((SUB, LANES), F32)
            for h in range(IDX_HEADS):
                acc = acc + wt_ref[h:h + 1, :] * jnp.maximum(dots_ref[slot * SUB:(slot + 1) * SUB, h * QB:(h + 1) * QB], 0.0)
            acc = jnp.where(acc == 0.0, 0.0, acc)
            sc = jnp.where((r0 + row_iota) <= qpos, acc, NEG)
            vals_ref[pl.ds(r0, SUB), :] = sc
            keys_ref[pl.ds(r0, SUB), :] = _sortable(sc)

        def scores(j, c):
            head_sum(2 * j, 0)
            head_dots(2 * j + 2, 0)
            head_sum(2 * j + 1, 1)
            head_dots(2 * j + 3, 1)
            return c

        head_dots(0, 0)
        head_dots(1, 1)
        lax.fori_loop(0, (nsub + 1) // 2, scores, 0)
        thr = _select_topk(keys_ref, vals_ref, cand_ref, nsub, qpos, topk, (qb * QB) // SUB)
        thr_ref[...] = jnp.broadcast_to(thr, thr_ref.shape)

        def clear(s, c):
            r0 = pl.multiple_of(s * SUB, SUB)
            keys_ref[pl.ds(r0, SUB), :] = jnp.full((SUB, LANES), INT_MIN, I32)
            return c

        lax.fori_loop(nsub, ((q_end + HT - 1) // HT) * (HT // SUB), clear, 0)
        m_ref[...] = jnp.full(m_ref.shape, NEG, F32)
        l_ref[...] = jnp.zeros(l_ref.shape, F32)
        acc_ref[...] = jnp.zeros(acc_ref.shape, F32)

    def attend(j):
        r0 = pl.multiple_of(kt * KT + j * HT, HT)
        bias_ref[...] = jnp.where(keys_ref[pl.ds(r0, HT), :] >= thr_ref[0:1, :], 0.0, 2 * NEG)
        parts = [slice(i * (HT // 2), (i + 1) * (HT // 2)) for i in range(2)]
        rows_of = lambda part: slice(j * HT + part.start, j * HT + part.stop)

        def qk(p):
            return [_dot_nt(k_ref[rows_of(pt), p * LANES:(p + 1) * LANES], qatt_ref[p]) for pt in parts]

        def softmax(h, sts):
            hh = h % 2
            scs = [st[:, hh * QB:(hh + 1) * QB] + bias_ref[pt, :] for st, pt in zip(sts, parts)]
            m_old = m_ref[h:h + 1, :]
            m_new = jnp.maximum(m_old, jnp.maximum(jnp.max(scs[0], axis=0, keepdims=True),
                                                   jnp.max(scs[1], axis=0, keepdims=True)))
            alpha = jnp.exp2(m_old - m_new)
            pms = [jnp.exp2(sc - m_new) for sc in scs]
            l_ref[h:h + 1, :] = (alpha * l_ref[h:h + 1, :] + jnp.sum(pms[0], axis=0, keepdims=True)
                                 + jnp.sum(pms[1], axis=0, keepdims=True))
            m_ref[h:h + 1, :] = m_new
            return alpha, [pm.astype(BF16) for pm in pms]

        def pv(h, alpha, pms):
            rows = slice(h * HEAD_DIM, (h + 1) * HEAD_DIM)
            out = _dot(vt_ref[rows, rows_of(parts[0])], pms[0]) + _dot(vt_ref[rows, rows_of(parts[1])], pms[1])
            acc_ref[rows, :] = alpha * acc_ref[rows, :] + out

        nxt = qk(0)
        pending = None
        for h in range(N_HEADS):
            if h % 2 == 0:
                sts = nxt
                if h + 2 < N_HEADS:
                    nxt = qk(h // 2 + 1)
            cur = (h,) + softmax(h, sts)
            if pending is not None:
                pv(*pending)
            pending = cur
        pv(*pending)

    attend(0)
    for part in range(1, KT // HT):
        pl.when(kt * KT + part * HT < q_end)(functools.partial(attend, part))

    @pl.when((kt + 1) * KT >= q_end)
    def _finish():
        for h in range(N_HEADS):
            rows = slice(h * HEAD_DIM, (h + 1) * HEAD_DIM)
            acc_ref[rows, :] = acc_ref[rows, :] / l_ref[h:h + 1, :]
        o_ref[...] = acc_ref[...].T


def _attn_prompt(qi_b, wi, ki_b, q_b, k_b, v_b, topk):
    t = qi_b.shape[0]
    nb = t // QB
    wt = wi.T
    vt = v_b.T

    steps = [(b, j) for b in range(nb) for j in range(((b + 1) * QB + KT - 1) // KT)]
    qb_tab = jnp.asarray(np.array([s[0] for s in steps], np.int32))
    kt_tab = jnp.asarray(np.array([s[1] for s in steps], np.int32))
    t_pad = ((t + KT - 1) // KT) * KT

    grid_spec = pltpu.PrefetchScalarGridSpec(
        num_scalar_prefetch=2,
        grid=(len(steps),),
        in_specs=[
            pl.BlockSpec((QB, IDX_HEADS * IDX_DIM), lambda i, qb, kt: (qb[i], 0)),
            pl.BlockSpec((IDX_HEADS, QB), lambda i, qb, kt: (0, qb[i])),
            pl.BlockSpec((t, IDX_DIM), lambda i, qb, kt: (0, 0)),
            pl.BlockSpec((QB, D_ATTN), lambda i, qb, kt: (qb[i], 0)),
            pl.BlockSpec((KT, D_ATTN), lambda i, qb, kt: (kt[i], 0)),
            pl.BlockSpec((D_ATTN, KT), lambda i, qb, kt: (0, kt[i])),
        ],
        out_specs=pl.BlockSpec((QB, D_ATTN), lambda i, qb, kt: (qb[i], 0)),
        scratch_shapes=[
            pltpu.VMEM((t_pad, LANES), I32),
            pltpu.VMEM((SUBLANES, LANES), I32),
            pltpu.VMEM((N_HEADS, LANES), F32),
            pltpu.VMEM((N_HEADS, LANES), F32),
            pltpu.VMEM((D_ATTN, LANES), F32),
            pltpu.VMEM((HT, LANES), F32),
            pltpu.VMEM((CAND_ROWS, LANES), I32),
            pltpu.VMEM((2 * SUB, IDX_HEADS * QB), F32),
            pltpu.VMEM((t_pad, LANES), F32),
            pltpu.VMEM((IDX_HEADS // 2, 2 * QB, IDX_DIM), BF16),
            pltpu.VMEM((N_HEADS // 2, 2 * QB, LANES), BF16),
        ],
    )
    return pl.pallas_call(
        functools.partial(_attn_prompt_kernel, topk=topk),
        grid_spec=grid_spec,
        out_shape=jax.ShapeDtypeStruct((t, D_ATTN), F32),
        compiler_params=_cparams(("arbitrary",)),
        name="attn_prompt",
    )(qb_tab, kt_tab, qi_b, wt, ki_b, q_b, k_b, vt)


def _mix_up_kernel(x_ref, cy_ref, ay_ref, gate_ref, shift_ref, scale_ref, gpm_ref, gpf_ref, wo_ref, wu_ref,
                   x1_ref, up_ref):
    mix = _dot(cy_ref[...].astype(BF16), wo_ref[0:D_CONV, :]) + _dot(ay_ref[...].astype(BF16), wo_ref[D_CONV:, :])
    x1 = x_ref[...] + gate_ref[...] * _rms(mix, gpm_ref[...])
    x1_ref[...] = x1
    h2 = (_rms(x1, gpf_ref[...]) * (1.0 + scale_ref[...]) + shift_ref[...]).astype(BF16)
    for c in range(up_ref.shape[1] // 512):
        up_ref[:, c * 512:(c + 1) * 512] = _dot(h2, wu_ref[:, c * 512:(c + 1) * 512])


def _mix_up(x, cy, ay, gate1, shift2, scale2, g_post_mix, g_pre_ffn, w_out, w_up, tm):
    r, d = x.shape
    nf = w_up.shape[1]
    per_row = gate1.shape[0] != 1
    mod_spec = pl.BlockSpec((tm, d), lambda i: (i, 0)) if per_row else pl.BlockSpec((1, d), lambda i: (0, 0))
    row = lambda n: pl.BlockSpec((tm, n), lambda i: (i, 0))
    full = lambda a: pl.BlockSpec(a.shape, lambda i: (0, 0))
    return pl.pallas_call(
        _mix_up_kernel,
        grid=(r // tm,),
        in_specs=[row(d), row(D_CONV), row(D_ATTN), mod_spec, mod_spec, mod_spec, full(g_post_mix), full(g_pre_ffn),
                  full(w_out), full(w_up)],
        out_specs=[row(d), row(nf)],
        out_shape=[jax.ShapeDtypeStruct((r, d), F32), jax.ShapeDtypeStruct((r, nf), F32)],
        compiler_params=_cparams(("arbitrary",)),
        name="mix_up",
    )(x, cy, ay, gate1, shift2, scale2, g_post_mix, g_pre_ffn, w_out, w_up)


def _ffn_down_kernel(*refs, bs, tm, hb, multi):
    if multi:
        hist_ref, halo_ref, main_ref, x1_ref, gate_ref, w_ref, b_ref, wd_ref, g_ref, o_ref, win_ref = refs
    else:
        hist_ref, main_ref, x1_ref, gate_ref, w_ref, b_ref, wd_ref, g_ref, o_ref, win_ref = refs
        halo_ref = None
    _fill_window(win_ref, hist_ref, halo_ref, main_ref, hb, tm)
    taps = w_ref.shape[0]
    base = hb - (taps - 1) * bs
    cw = 256

    def conv(c0):
        acc = jnp.broadcast_to(b_ref[:, c0:c0 + cw], (tm, cw))
        for j in range(taps):
            acc = acc + w_ref[j:j + 1, c0:c0 + cw] * win_ref[base + j * bs:base + j * bs + tm, c0:c0 + cw]
        return acc

    f = jnp.zeros((tm, o_ref.shape[1]), F32)
    for c in range(D_FF // cw):
        a = conv(c * cw)
        g = conv(D_FF + c * cw)
        gated = (g * jax.nn.sigmoid(g) * a).astype(BF16)
        f = f + _dot(gated, wd_ref[c * cw:(c + 1) * cw, :])
    o_ref[...] = x1_ref[...] + gate_ref[...] * _rms(f, g_ref[...])


def _ffn_down(hist, up, x1, gate2, w, b, w_down, g, *, bs, tm):
    r, nf = up.shape
    d = x1.shape[1]
    hb = hist.shape[0]
    n_tiles = r // tm
    per_row = gate2.shape[0] != 1
    mod_spec = pl.BlockSpec((tm, d), lambda i: (i, 0)) if per_row else pl.BlockSpec((1, d), lambda i: (0, 0))
    full = lambda a: pl.BlockSpec(a.shape, lambda i: (0, 0))
    kern = functools.partial(_ffn_down_kernel, bs=bs, tm=tm, hb=hb, multi=n_tiles > 1)
    args = [hist] + ([up] if n_tiles > 1 else []) + [up, x1, gate2, w, b, w_down, g]
    return pl.pallas_call(
        kern,
        grid=(n_tiles,),
        in_specs=_window_specs(hist, nf, tm, hb, n_tiles) + [pl.BlockSpec((tm, d), lambda i: (i, 0)), mod_spec,
                                                              full(w), full(b), full(w_down), full(g)],
        out_specs=pl.BlockSpec((tm, d), lambda i: (i, 0)),
        out_shape=jax.ShapeDtypeStruct((r, d), F32),
        scratch_shapes=[pltpu.VMEM((hb + tm, nf), F32)],
        compiler_params=_cparams(("arbitrary",)),
        name="ffn_down",
    )(*args)


def _sample_scores_kernel(pt_ref, q_ref, w_ref, knew_ref, *refs, n_chunks, past, n_new):
    page_refs, o_ref = refs[:PPC], refs[PPC]
    c = pl.program_id(1)
    q = q_ref[...]
    w = w_ref[...]

    def head_sum(page):
        sc = _dot(q, page.astype(BF16))
        val = w * jnp.maximum(sc, 0.0)
        tot = val.reshape(IDX_HEADS, SUBLANES, LANES).sum(axis=0)
        return jnp.where(tot == 0.0, 0.0, tot)

    tq = lax.broadcasted_iota(I32, (SUBLANES, LANES), 0)
    j = lax.broadcasted_iota(I32, (SUBLANES, LANES), 1)

    @pl.when(c < n_chunks)
    def _past():
        for r in range(PPC):
            pos = (c * PPC + r) * PAGE_SIZE + j
            o_ref[:, r * LANES:(r + 1) * LANES] = jnp.where(tq < n_new, head_sum(page_refs[r][...]), -pos.astype(F32))

    @pl.when(c == n_chunks)
    def _new():
        valid = jnp.logical_and(j <= tq, j < n_new)
        o_ref[:, 0:LANES] = jnp.where(valid, head_sum(knew_ref[...]), NEG)
        o_ref[:, LANES:] = jnp.full((SUBLANES, (PPC - 1) * LANES), NEG, F32)


def _sample_scores(page_table, q_rows, w_rows, kit_new, cache_kit, layer):
    nb, n_pages = page_table.shape
    n_chunks = n_pages // PPC
    past = n_pages * PAGE_SIZE
    cw = PPC * PAGE_SIZE

    def page_spec(r):
        return pl.BlockSpec((None, None, IDX_DIM, PAGE_SIZE),
                            lambda b, c, pt: (layer, pt[b, jnp.minimum(c * PPC + r, n_pages - 1)], 0, 0))

    grid_spec = pltpu.PrefetchScalarGridSpec(
        num_scalar_prefetch=1,
        grid=(nb, n_chunks + 1),
        in_specs=[pl.BlockSpec((None, IDX_HEADS * SUBLANES, IDX_DIM), lambda b, c, pt: (b, 0, 0)),
                  pl.BlockSpec((None, IDX_HEADS * SUBLANES, LANES), lambda b, c, pt: (b, 0, 0)),
                  pl.BlockSpec((None, IDX_DIM, PAGE_SIZE), lambda b, c, pt: (b, 0, 0))]
                 + [page_spec(r) for r in range(PPC)],
        out_specs=pl.BlockSpec((None, SUBLANES, cw), lambda b, c, pt: (b, 0, c)),
    )
    return pl.pallas_call(
        functools.partial(_sample_scores_kernel, n_chunks=n_chunks, past=past, n_new=4),
        grid_spec=grid_spec,
        out_shape=jax.ShapeDtypeStruct((nb, SUBLANES, past + cw), F32),
        compiler_params=_cparams(("arbitrary", "arbitrary")),
        name="sample_scores",
    )(page_table, q_rows, w_rows, kit_new, *([cache_kit] * PPC))


def _sample_select_kernel(sc_ref, lim_ref, keys_ref, thr_ref, cand_ref, *, topk):
    rows = sc_ref.shape[0]

    def conv(s, c):
        r0 = pl.multiple_of(s * SUB, SUB)
        keys_ref[pl.ds(r0, SUB), :] = _sortable(sc_ref[pl.ds(r0, SUB), :])
        return c

    lax.fori_loop(0, rows // SUB, conv, 0)
    thr = _select_topk(keys_ref, None, cand_ref, rows // SUB, lim_ref[...], topk, 0)
    thr_ref[...] = jnp.broadcast_to(thr, thr_ref.shape)


def _sample_select(scores_t, limit, topk):
    rows, nq = scores_t.shape
    return pl.pallas_call(
        functools.partial(_sample_select_kernel, topk=topk),
        grid=(nq // LANES,),
        in_specs=[pl.BlockSpec((rows, LANES), lambda g: (0, g)), pl.BlockSpec((1, LANES), lambda g: (0, g))],
        out_specs=[pl.BlockSpec((rows, LANES), lambda g: (0, g)), pl.BlockSpec((SUBLANES, LANES), lambda g: (0, g))],
        out_shape=[jax.ShapeDtypeStruct((rows, nq), I32), jax.ShapeDtypeStruct((SUBLANES, nq), I32)],
        scratch_shapes=[pltpu.VMEM((CAND_ROWS, LANES), I32)],
        compiler_params=_cparams(("arbitrary",)),
        name="sample_select",
    )(scores_t, limit)


def _sample_attn_kernel(pt_ref, q_ref, keys_ref, thr_ref, knew_ref, vnew_ref, *refs, n_chunks):
    k_refs, v_refs = refs[:APC], refs[APC:2 * APC]
    o_ref, m_ref, l_ref, acc_ref = refs[2 * APC:]
    c = pl.program_id(1)

    @pl.when(c == 0)
    def _init():
        m_ref[...] = jnp.full(m_ref.shape, NEG, F32)
        l_ref[...] = jnp.zeros(l_ref.shape, F32)
        acc_ref[...] = jnp.zeros(acc_ref.shape, F32)

    thr = thr_ref[...]

    def pages_update(pages):
        n = len(pages)
        bias = jnp.where(keys_ref[:, 0:n * LANES] >= thr[:, 0:1], 0.0, 2 * NEG)
        scs = []
        for h in range(N_HEADS):
            kt = jnp.concatenate([kp[h].astype(BF16) for kp, _ in pages], axis=1)
            scs.append(_dot(q_ref[h * SUBLANES:(h + 1) * SUBLANES, :], kt) + bias)
        sc = jnp.concatenate(scs, axis=0)
        m_old = m_ref[...]
        m_new = jnp.maximum(m_old, jnp.max(sc, axis=1, keepdims=True))
        alpha = jnp.exp2(m_old - m_new)
        pm = jnp.exp2(sc - m_new)
        l_ref[...] = alpha * l_ref[...] + jnp.sum(pm, axis=1, keepdims=True)
        pb = pm.astype(BF16)
        outs = []
        for h in range(N_HEADS):
            vt = jnp.concatenate([vp[h].astype(BF16) for _, vp in pages], axis=1)
            outs.append(_dot_nt(pb[h * SUBLANES:(h + 1) * SUBLANES, :], vt))
        acc_ref[...] = alpha * acc_ref[...] + jnp.concatenate(outs, axis=0)
        m_ref[...] = m_new

    @pl.when(c < n_chunks)
    def _past():
        pages_update(list(zip(k_refs, v_refs)))

    @pl.when(c == n_chunks)
    def _new():
        pages_update([(knew_ref, vnew_ref)])
        out = acc_ref[...] / l_ref[...]
        o_ref[...] = jnp.concatenate([out[h * SUBLANES:(h + 1) * SUBLANES, :] for h in range(N_HEADS)], axis=1)


def _sample_attn(page_table, q_rows, keys, thr, kt_new, vt_new, cache_kt, cache_vt, layer):
    nb, n_pages = page_table.shape
    n_chunks = n_pages // APC
    cw = APC * PAGE_SIZE
    rows = N_HEADS * SUBLANES

    def page_spec(r):
        return pl.BlockSpec((None, None, N_HEADS, HEAD_DIM, PAGE_SIZE),
                            lambda b, c, pt: (layer, pt[b, jnp.minimum(c * APC + r, n_pages - 1)], 0, 0, 0))

    new_spec = pl.BlockSpec((None, N_HEADS, HEAD_DIM, PAGE_SIZE), lambda b, c, pt: (b, 0, 0, 0))
    grid_spec = pltpu.PrefetchScalarGridSpec(
        num_scalar_prefetch=1,
        grid=(nb, n_chunks + 1),
        in_specs=[pl.BlockSpec((None, rows, HEAD_DIM), lambda b, c, pt: (b, 0, 0)),
                  pl.BlockSpec((None, SUBLANES, cw), lambda b, c, pt: (b, 0, c)),
                  pl.BlockSpec((None, SUBLANES, LANES), lambda b, c, pt: (b, 0, 0)),
                  new_spec, new_spec]
                 + [page_spec(r) for r in range(APC)] * 2,
        out_specs=pl.BlockSpec((None, SUBLANES, D_ATTN), lambda b, c, pt: (b, 0, 0)),
        scratch_shapes=[pltpu.VMEM((rows, 1), F32), pltpu.VMEM((rows, 1), F32), pltpu.VMEM((rows, HEAD_DIM), F32)],
    )
    return pl.pallas_call(
        functools.partial(_sample_attn_kernel, n_chunks=n_chunks),
        grid_spec=grid_spec,
        out_shape=jax.ShapeDtypeStruct((nb, SUBLANES, D_ATTN), F32),
        compiler_params=_cparams(("arbitrary", "arbitrary")),
        name="sample_attn",
    )(page_table, q_rows, keys, thr, kt_new, vt_new, *([cache_kt] * APC), *([cache_vt] * APC))


def _split_mod(mod):
    return [mod[:, i * D_MODEL:(i + 1) * D_MODEL] for i in range(6)]


def _layer_prompt(x, mod, wts):
    t = x.shape[0]
    shift1, scale1, gate1, shift2, scale2, gate2 = _split_mod(mod)
    glu, q_b, k, k_b, v, v_b, qi_b, tail = _inproj(x, shift1, scale1, wts["g_pre_mix"], wts["w_main"], wts["w_tail"], 512)
    ki = tail[:, :IDX_DIM]
    wi = tail[:, IDX_DIM:IDX_DIM + IDX_HEADS]
    conv_hist = jnp.zeros((32, D_CONV), F32)
    conv_y = _conv_ln(conv_hist, glu, wts["w_dw"], wts["b_dw"], wts["ln_g"], wts["ln_b"], bs=1, tm=256, rc=32)
    attn_y = _attn_prompt(qi_b, wi, ki.astype(BF16), q_b, k_b, v_b, min(TOPK_MAX, t // 4))
    x1, up = _mix_up(x, conv_y, attn_y, gate1, shift2, scale2, wts["g_post_mix"], wts["g_pre_ffn"],
                     wts["w_out"], wts["w_up"], 512)
    ffn_hist = jnp.zeros((SUBLANES, 2 * D_FF), F32)
    y = _ffn_down(ffn_hist, up, x1, gate2, wts["w_ffn_dw"], wts["b_ffn_dw"], wts["w_down"], wts["g_post_ffn"],
                  bs=1, tm=256)
    return y, k, v, ki, glu[t - (CONV_WIDTH - 1):], up[t - (FFN_CONV_WIDTH - 1):]


def _layer_sample(x, mod, conv_state, ffn_state, cache_kt, cache_vt, cache_kit, page_table, layer, wts):
    nb, n_pages = page_table.shape
    r = x.shape[0]
    nt = r // nb
    past = n_pages * PAGE_SIZE
    shift1, scale1, gate1, shift2, scale2, gate2 = [jnp.tile(m, (nt, 1)) for m in _split_mod(mod)]
    glu, q_b, k, k_b, v, v_b, qi_b, tail = _inproj(x, shift1, scale1, wts["g_pre_mix"], wts["w_main"], wts["w_tail"], r)
    ki = tail[:, :IDX_DIM]
    wi = tail[:, IDX_DIM:IDX_DIM + IDX_HEADS]
    conv_hist = conv_state.transpose(1, 0, 2).reshape((CONV_WIDTH - 1) * nb, D_CONV)
    conv_y = _conv_ln(conv_hist, glu, wts["w_dw"], wts["b_dw"], wts["ln_g"], wts["ln_b"], bs=nb, tm=r, rc=32)

    tb = lambda a: a.reshape(nt, nb, -1).transpose(1, 0, 2)
    pad_t = lambda a: jnp.pad(a, ((0, 0), (0, SUBLANES - nt)) + ((0, 0),) * (a.ndim - 2))
    qi_s = pad_t(tb(qi_b).reshape(nb, nt, IDX_HEADS, IDX_DIM)).transpose(0, 2, 1, 3)
    q_rows = qi_s.reshape(nb, IDX_HEADS * SUBLANES, IDX_DIM)
    w_s = pad_t(tb(wi)).transpose(0, 2, 1).reshape(nb, IDX_HEADS * SUBLANES, 1)
    w_rows = jnp.broadcast_to(w_s, (nb, IDX_HEADS * SUBLANES, LANES))
    pad_keys = lambda a: jnp.pad(tb(a), ((0, 0), (0, PAGE_SIZE - nt), (0, 0)))
    kit_new = pad_keys(ki).transpose(0, 2, 1)
    scores = _sample_scores(page_table, q_rows, w_rows, kit_new, cache_kit, layer)
    total = scores.shape[2]
    topk = min(TOPK_MAX, (past + nt) // 4)
    scores_t = scores.reshape(nb * SUBLANES, total).T
    limit = jnp.tile(past + jnp.arange(SUBLANES, dtype=I32), nb).reshape(1, nb * SUBLANES)
    keys_t, thr = _sample_select(scores_t, limit, topk)
    keys = keys_t.T.reshape(nb, SUBLANES, total)
    thr_rows = jnp.broadcast_to(thr[0].reshape(nb, SUBLANES, 1), (nb, SUBLANES, LANES))

    q_s = pad_t(tb(q_b).reshape(nb, nt, N_HEADS, HEAD_DIM)).transpose(0, 2, 1, 3)
    new_t = lambda a: pad_keys(a).reshape(nb, PAGE_SIZE, N_HEADS, HEAD_DIM).transpose(0, 2, 3, 1)
    attn = _sample_attn(page_table, q_s.reshape(nb, N_HEADS * SUBLANES, HEAD_DIM), keys, thr_rows,
                        new_t(k), new_t(v), cache_kt, cache_vt, layer)
    attn_y = attn[:, :nt].transpose(1, 0, 2).reshape(r, D_ATTN)

    x1, up = _mix_up(x, conv_y, attn_y, gate1, shift2, scale2, wts["g_post_mix"], wts["g_pre_ffn"],
                     wts["w_out"], wts["w_up"], r)
    ffn_hist = ffn_state.transpose(1, 0, 2).reshape((FFN_CONV_WIDTH - 1) * nb, 2 * D_FF)
    y = _ffn_down(ffn_hist, up, x1, gate2, wts["w_ffn_dw"], wts["b_ffn_dw"], wts["w_down"], wts["g_post_ffn"],
                  bs=nb, tm=r)
    conv_all = jnp.concatenate([conv_hist, glu], axis=0).reshape(CONV_WIDTH - 1 + nt, nb, D_CONV)
    ffn_all = jnp.concatenate([ffn_hist, up], axis=0).reshape(FFN_CONV_WIDTH - 1 + nt, nb, 2 * D_FF)
    return (y, k, v, ki, conv_all[nt:].transpose(1, 0, 2), ffn_all[nt:].transpose(1, 0, 2))


def kernel(x_prompt, x_sample, cache_k, cache_v, cache_kidx, state_conv, state_ffn, page_table, c_prompt, c_sample, w_ada, b_ada, g_pre_mix, w_in, w_dw, b_dw, ln_g, ln_b, w_out, g_post_mix, g_pre_ffn, w_up, w_ffn_dw, b_ffn_dw, w_down, g_post_ffn):
    depth = w_ada.shape[0]
    pb, seq, d = x_prompt.shape
    db, dt, _ = x_sample.shape
    assert pb == 1 and d == D_MODEL and seq % KT == 0 and db * dt == LANES and dt <= SUBLANES
    cache_kt = cache_k.transpose(0, 1, 3, 4, 2)
    cache_vt = cache_v.transpose(0, 1, 3, 4, 2)
    cache_kit = cache_kidx.transpose(0, 1, 3, 2)

    xp = x_prompt[0]
    xs = x_sample.transpose(1, 0, 2).reshape(dt * db, d)
    n_c = pb + db
    c_all = jnp.pad(jnp.concatenate([c_prompt, c_sample], axis=0), ((0, (-n_c) % SUBLANES), (0, 0)))
    row = lambda a: a.reshape(1, -1)
    outs_p, outs_s = [], []
    for l in range(depth):
        mod = _ada(c_all, w_ada[l], b_ada[l])
        wts = dict(
            g_pre_mix=row(g_pre_mix[l]), w_main=w_in[l][:, :N_MAIN].astype(BF16),
            w_tail=jnp.pad(w_in[l][:, N_MAIN:], ((0, 0), (0, LANES - IDX_DIM - IDX_HEADS))).astype(BF16),
            w_dw=w_dw[l], b_dw=row(b_dw[l]), ln_g=row(ln_g[l]), ln_b=row(ln_b[l]),
            w_out=w_out[l].astype(BF16), g_post_mix=row(g_post_mix[l]), g_pre_ffn=row(g_pre_ffn[l]),
            w_up=w_up[l].astype(BF16), w_ffn_dw=w_ffn_dw[l], b_ffn_dw=row(b_ffn_dw[l]),
            w_down=w_down[l].astype(BF16), g_post_ffn=row(g_post_ffn[l]))
        xp, kp, vp, kip, cp, fp = _layer_prompt(xp, mod[:pb], wts)
        xs, k_s, v_s, ki_s, c_s, f_s = _layer_sample(xs, mod[pb:n_c], state_conv[l], state_ffn[l], cache_kt, cache_vt,
                                                      cache_kit, page_table, l, wts)
        unflat = lambda a, n: a.reshape(dt, db, n).transpose(1, 0, 2)
        outs_p.append((kp.reshape(pb, seq, N_HEADS, HEAD_DIM), vp.reshape(pb, seq, N_HEADS, HEAD_DIM),
                       kip.reshape(pb, seq, IDX_DIM), cp[None], fp[None]))
        outs_s.append((unflat(k_s, D_ATTN).reshape(db, dt, N_HEADS, HEAD_DIM),
                       unflat(v_s, D_ATTN).reshape(db, dt, N_HEADS, HEAD_DIM), unflat(ki_s, IDX_DIM), c_s, f_s))
    stack = lambda outs, i: jnp.stack([o[i] for o in outs])
    y_prompt = xp[None]
    y_sample = xs.reshape(dt, db, d).transpose(1, 0, 2)
    return (y_prompt, y_sample) + tuple(stack(outs_p, i) for i in range(5)) + tuple(stack(outs_s, i) for i in range(5))
```
